```python
import math
import jax, jax.numpy as jnp
from jax import lax
import numpy as np

D_MODEL = 1024
BATCH = 8
SEQ = 4096
DEPTH = 2

N_MIXERS = 2
DN_HEADS = 8
DN_HEAD_DIM = 128
DN_DIM = DN_HEADS * DN_HEAD_DIM
DN_CONV = 4
DN_CHUNK = 64
DN_IN = 4 * DN_DIM + 2 * DN_HEADS
MB_HEADS = 8
MB_HEAD_DIM = D_MODEL // MB_HEADS
MB_DIM = MB_HEADS * MB_HEAD_DIM
MB_BLOCK = 256
MB_TOPK = 3
MB_QCHUNK = 16
D_FF = 4 * D_MODEL
RMS_EPS = 1e-6
N_DN_LAYERS = (DEPTH + 1) // 2
N_MB_LAYERS = DEPTH // 2

kernel_name = "hybrid_gdn_moba_alibi_sqrelu"


def rmsnorm(x, g):
    xf = x.astype(jnp.float32)
    y = xf * lax.rsqrt(jnp.mean(xf * xf, axis=-1, keepdims=True) + RMS_EPS)
    return (y * g.astype(jnp.float32)).astype(x.dtype)


def l2norm(x):
    xf = x.astype(jnp.float32)
    return xf * lax.rsqrt(jnp.sum(xf * xf, axis=-1, keepdims=True) + 1e-6)


def causal_depthwise_conv(x, w):
    k_w = w.shape[0]
    t = x.shape[1]
    xp = jnp.pad(x, ((0, 0), (k_w - 1, 0), (0, 0)))
    y = xp[:, 0:t] * w[0]
    for j in range(1, k_w):
        y = y + xp[:, j:j + t] * w[j]
    return y


def alibi_slopes(n_heads):
    return jnp.exp2(-8.0 * jnp.arange(1, n_heads + 1, dtype=jnp.float32) / n_heads)


def chunk_gated_delta_rule(q, k, v, g, beta):
    b, h, t, dk = q.shape
    dv = v.shape[-1]
    c = DN_CHUNK
    n = t // c
    q = q.astype(jnp.float32).reshape(b, h, n, c, dk)
    k = k.astype(jnp.float32).reshape(b, h, n, c, dk)
    v = v.astype(jnp.float32).reshape(b, h, n, c, dv)
    g = g.astype(jnp.float32).reshape(b, h, n, c)
    beta = beta.astype(jnp.float32).reshape(b, h, n, c)

    gc = jnp.cumsum(g, axis=-1)
    idx = jnp.arange(c)
    tri_incl = idx[:, None] >= idx[None, :]
    tri_strict = idx[:, None] > idx[None, :]
    decay = jnp.exp(jnp.where(tri_incl, gc[..., :, None] - gc[..., None, :], -jnp.inf))

    k_beta = k * beta[..., None]
    a_low = jnp.where(tri_strict, jnp.einsum('bhncd,bhnsd->bhncs', k_beta, k) * decay, 0.0)
    a_mat = a_low + jnp.eye(c, dtype=jnp.float32)
    rhs = jnp.concatenate([v * beta[..., None], k_beta * jnp.exp(gc)[..., None]], axis=-1)
    sol = lax.linalg.triangular_solve(a_mat, rhs, left_side=True, lower=True, unit_diagonal=True)
    u, w = sol[..., :dv], sol[..., dv:]

    attn_intra = jnp.einsum('bhncd,bhnsd->bhncs', q, k) * decay
    q_dec = q * jnp.exp(gc)[..., None]
    k_dec = k * jnp.exp(gc[..., -1:] - gc)[..., None]
    g_tot = jnp.exp(gc[..., -1])

    xs = tuple(jnp.moveaxis(a, 2, 0) for a in (u, w, q_dec, k_dec, attn_intra, g_tot))

    def step(state, inp):
        u_c, w_c, qd_c, kd_c, at_c, gt_c = inp
        v_new = u_c - jnp.einsum('bhcd,bhde->bhce', w_c, state)
        o_c = jnp.einsum('bhcd,bhde->bhce', qd_c, state) + jnp.einsum('bhcs,bhse->bhce', at_c, v_new)
        state = state * gt_c[..., None, None] + jnp.einsum('bhcd,bhce->bhde', kd_c, v_new)
        return state, o_c

    s0 = jnp.zeros((b, h, dk, dv), jnp.float32)
    _, o = lax.scan(step, s0, xs)
    return jnp.moveaxis(o, 0, 2).reshape(b, h, t, dv)


def gated_deltanet(hn, w_in, conv_w, a_log, dt_bias, out_norm_g, w_out):
    b, t, _ = hn.shape
    proj = hn @ w_in
    qkv = jax.nn.silu(causal_depthwise_conv(proj[..., :3 * DN_DIM], conv_w))
    z = proj[..., 3 * DN_DIM:4 * DN_DIM]
    b_raw = proj[..., 4 * DN_DIM:4 * DN_DIM + DN_HEADS].astype(jnp.float32)
    a_raw = proj[..., 4 * DN_DIM + DN_HEADS:].astype(jnp.float32)

    def heads(a):
        return a.reshape(b, t, DN_HEADS, DN_HEAD_DIM).transpose(0, 2, 1, 3)

    q = heads(l2norm(heads(qkv[..., :DN_DIM]).transpose(0, 2, 1, 3)) * DN_HEAD_DIM ** -0.5)
    k = heads(l2norm(heads(qkv[..., DN_DIM:2 * DN_DIM]).transpose(0, 2, 1, 3)))
    v = heads(qkv[..., 2 * DN_DIM:])
    beta = jax.nn.sigmoid(b_raw).transpose(0, 2, 1)
    g = (-jnp.exp(a_log.astype(jnp.float32)) *
         jax.nn.softplus(a_raw + dt_bias.astype(jnp.float32))).transpose(0, 2, 1)
    o = chunk_gated_delta_rule(q, k, v, g, beta)
    o = o.transpose(0, 2, 1, 3).astype(hn.dtype)
    o = rmsnorm(o, out_norm_g) * jax.nn.silu(z.reshape(b, t, DN_HEADS, DN_HEAD_DIM))
    return o.reshape(b, t, DN_DIM) @ w_out


def moba_attention(q, k, v):
    b, h, t, dh = q.shape
    nb = -(-t // MB_BLOCK)
    tp = nb * MB_BLOCK
    pad = ((0, 0), (0, 0), (0, tp - t), (0, 0))
    q, k, v = jnp.pad(q, pad), jnp.pad(k, pad), jnp.pad(v, pad)
    kb = k.reshape(b, h, nb, MB_BLOCK, dh)
    vb = v.reshape(b, h, nb, MB_BLOCK, dh)
    top = min(MB_TOPK, nb)
    scale = dh ** -0.5
    slopes = alibi_slopes(h)

    k_mean = jnp.mean(kb.astype(jnp.float32), axis=3)
    gate = jnp.einsum('bhtd,bhnd->bhtn', q.astype(jnp.float32), k_mean)
    q_blk = jnp.arange(tp) // MB_BLOCK
    past = jnp.arange(nb)[None, :] < q_blk[:, None]
    gate = jnp.where(past, gate, -jnp.inf)
    _, sel = lax.top_k(gate, top)
    valid = sel < q_blk[:, None]

    n_qc = tp // MB_QCHUNK

    def to_chunks(a):
        return jnp.moveaxis(a.reshape(b, h, n_qc, MB_QCHUNK, *a.shape[3:]), 2, 0)

    gather_blocks = jax.vmap(jax.vmap(lambda blocks, ids: blocks[ids]))

    def attend_chunk(args):
        qc, selc, validc, ci = args
        t0 = ci * MB_QCHUNK
        blk = t0 // MB_BLOCK
        tpos = (t0 + jnp.arange(MB_QCHUNK)).astype(jnp.float32)
        ks = gather_blocks(kb, selc)
        vs = gather_blocks(vb, selc)
        s_sel = jnp.einsum('bhqd,bhqnld->bhqnl', qc, ks).astype(jnp.float32) * scale
        kpos_sel = (selc[..., None] * MB_BLOCK + jnp.arange(MB_BLOCK)).astype(jnp.float32)
        dist_sel = tpos[:, None, None] - kpos_sel
        s_sel = jnp.where(validc[..., None], s_sel - slopes[:, None, None, None] * dist_sel, -jnp.inf)
        ko = lax.dynamic_index_in_dim(kb, blk, axis=2, keepdims=False)
        vo = lax.dynamic_index_in_dim(vb, blk, axis=2, keepdims=False)
        s_own = jnp.einsum('bhqd,bhld->bhql', qc, ko).astype(jnp.float32) * scale
        dist_own = tpos[:, None] - (blk * MB_BLOCK + jnp.arange(MB_BLOCK)).astype(jnp.float32)
        s_own = jnp.where(dist_own >= 0, s_own - slopes[:, None, None] * dist_own, -jnp.inf)
        logits = jnp.concatenate([s_sel.reshape(b, h, MB_QCHUNK, top * MB_BLOCK), s_own], axis=-1)
        p = jax.nn.softmax(logits, axis=-1).astype(vs.dtype)
        p_sel = p[..., :top * MB_BLOCK].reshape(b, h, MB_QCHUNK, top, MB_BLOCK)
        p_own = p[..., top * MB_BLOCK:]
        return (jnp.einsum('bhqnl,bhqnld->bhqd', p_sel, vs) +
                jnp.einsum('bhql,bhld->bhqd', p_own, vo))

    out = lax.map(attend_chunk, (to_chunks(q), to_chunks(sel), to_chunks(valid), jnp.arange(n_qc)))
    out = jnp.moveaxis(out, 0, 2).reshape(b, h, tp, dh)
    return out[:, :, :t]


def moba_layer(hn, w_in, w_out):
    b, t, _ = hn.shape
    proj = hn @ w_in

    def heads(a):
        return a.reshape(b, t, MB_HEADS, MB_HEAD_DIM).transpose(0, 2, 1, 3)

    q = heads(proj[..., :MB_DIM])
    k = heads(proj[..., MB_DIM:2 * MB_DIM])
    v = heads(proj[..., 2 * MB_DIM:])
    o = moba_attention(q, k, v)
    return o.transpose(0, 2, 1, 3).reshape(b, t, MB_DIM) @ w_out


def squared_relu_mlp(hn, w_up, w_down):
    return jnp.square(jax.nn.relu(hn @ w_up)) @ w_down


def setup_inputs(seed: int = 0) -> dict:
    key = jax.random.key(seed)
    ks = jax.random.split(key, 16)
    f32 = jnp.float32
    x = jax.random.normal(ks[0], (BATCH, SEQ, D_MODEL), f32)
    mix_norm_g = 1.0 + 0.02 * jax.random.normal(ks[1], (DEPTH, D_MODEL), f32)
    mlp_norm_g = 1.0 + 0.02 * jax.random.normal(ks[2], (DEPTH, D_MODEL), f32)
    dn_w_in = jax.random.normal(ks[3], (N_DN_LAYERS, D_MODEL, DN_IN), f32) * D_MODEL ** -0.5
    dn_conv_w = jax.random.normal(ks[4], (N_DN_LAYERS, DN_CONV, 3 * DN_DIM), f32) * DN_CONV ** -0.5
    dn_a_log = jnp.log(jax.random.uniform(ks[5], (N_DN_LAYERS, DN_HEADS), f32, 1.0, 16.0))
    dt = jnp.exp(jax.random.uniform(ks[6], (N_DN_LAYERS, DN_HEADS), f32, math.log(1e-3), math.log(1e-1)))
    dn_dt_bias = dt + jnp.log(-jnp.expm1(-dt))
    dn_out_norm_g = 1.0 + 0.02 * jax.random.normal(ks[7], (N_DN_LAYERS, DN_HEAD_DIM), f32)
    dn_w_out = jax.random.normal(ks[8], (N_DN_LAYERS, DN_DIM, D_MODEL), f32) * DN_DIM ** -0.5
    mb_w_in = jax.random.normal(ks[9], (N_MB_LAYERS, D_MODEL, 3 * MB_DIM), f32) * D_MODEL ** -0.5
    mb_w_out = jax.random.normal(ks[10], (N_MB_LAYERS, MB_DIM, D_MODEL), f32) * MB_DIM ** -0.5
    mlp_w_up = jax.random.normal(ks[11], (DEPTH, D_MODEL, D_FF), f32) * D_MODEL ** -0.5
    mlp_w_down = jax.random.normal(ks[12], (DEPTH, D_FF, D_MODEL), f32) * D_FF ** -0.5
    final_norm_g = 1.0 + 0.02 * jax.random.normal(ks[13], (D_MODEL,), f32)
    return {"x": x, "mix_norm_g": mix_norm_g, "mlp_norm_g": mlp_norm_g,
            "dn_w_in": dn_w_in, "dn_conv_w": dn_conv_w, "dn_a_log": dn_a_log,
            "dn_dt_bias": dn_dt_bias, "dn_out_norm_g": dn_out_norm_g, "dn_w_out": dn_w_out,
            "mb_w_in": mb_w_in, "mb_w_out": mb_w_out,
            "mlp_w_up": mlp_w_up, "mlp_w_down": mlp_w_down, "final_norm_g": final_norm_g}


def reference(x, mix_norm_g, mlp_norm_g, dn_w_in, dn_conv_w, dn_a_log, dn_dt_bias,
              dn_out_norm_g, dn_w_out, mb_w_in, mb_w_out, mlp_w_up, mlp_w_down, final_norm_g):
    h = x
    for i in range(DEPTH):
        hn = rmsnorm(h, mix_norm_g[i])
        j = i // N_MIXERS
        if i % N_MIXERS == 0:
            mix = gated_deltanet(hn, dn_w_in[j], dn_conv_w[j], dn_a_log[j], dn_dt_bias[j],
                                 dn_out_norm_g[j], dn_w_out[j])
        else:
            mix = moba_layer(hn, mb_w_in[j], mb_w_out[j])
        h = h + mix.astype(h.dtype)
        h = h + squared_relu_mlp(rmsnorm(h, mlp_norm_g[i]), mlp_w_up[i], mlp_w_down[i])
    return rmsnorm(h, final_norm_g)
```

```python
import functools
import math

import jax
import jax.numpy as jnp
from jax import lax
from jax.experimental import pallas as pl
from jax.experimental.pallas import tpu as pltpu

F32 = jnp.float32
BF16 = jnp.bfloat16

RMS_EPS = 1e-6
L2_EPS = 1e-6

DN_HEADS = 8
DN_HEAD_DIM = 128
DN_CONV = 4
GDN_CHUNK = 128
MB_HEADS = 8
MB_BLOCK = 256
MB_TOPK = 3

LANES = 128
SUBLANES = 8
VMEM_LIMIT_BYTES = 56 * 1024 * 1024

ROW_TILE = 512
COL_TILE = 512
GDN_TIME_TILE = 256


def _dot(a, b):
    return jnp.dot(a, b, preferred_element_type=F32)


def _dot_nt(a, b):
    return lax.dot_general(a, b, (((1,), (1,)), ((), ())), preferred_element_type=F32)


def _dot_tn(a, b):
    return lax.dot_general(a, b, (((0,), (0,)), ((), ())), preferred_element_type=F32)


def _rms_scale(x):
    return x * lax.rsqrt(jnp.mean(x * x, axis=-1, keepdims=True) + RMS_EPS)


def _const_spec(shape):
    return pl.BlockSpec(shape, lambda *_: (0,) * len(shape), pipeline_mode=pl.Buffered(1))


def _rms_proj_kernel(x_ref, g_ref, *refs, n_w):
    w_refs, o_refs = refs[:n_w], refs[n_w:]
    hn = (_rms_scale(x_ref[...]) * g_ref[...]).astype(BF16)
    for w_ref, o_ref in zip(w_refs, o_refs):
        n_cols = w_ref.shape[1]
        tn = min(COL_TILE, n_cols)
        for c in range(n_cols // tn):
            cols = slice(c * tn, (c + 1) * tn)
            o_ref[:, cols] = _dot(hn, w_ref[:, cols]).astype(o_ref.dtype)


def _rms_proj(x, g, ws, out_dtypes):
    n, d = x.shape
    tm = min(ROW_TILE, n)
    in_specs = [pl.BlockSpec((tm, d), lambda i: (i, 0)), _const_spec((1, d))]
    in_specs += [_const_spec(w.shape) for w in ws]
    out_specs = [pl.BlockSpec((tm, w.shape[1]), lambda i: (i, 0)) for w in ws]
    out_shape = [jax.ShapeDtypeStruct((n, w.shape[1]), dt) for w, dt in zip(ws, out_dtypes)]
    return pl.pallas_call(
        functools.partial(_rms_proj_kernel, n_w=len(ws)),
        grid=(n // tm,),
        in_specs=in_specs,
        out_specs=out_specs,
        out_shape=out_shape,
        compiler_params=pltpu.CompilerParams(
            dimension_semantics=("parallel",), vmem_limit_bytes=VMEM_LIMIT_BYTES),
        name="rms_proj",
    )(x, g, *ws)


def _out_mlp_kernel(o_ref, h_ref, wout_ref, g_ref, wup_ref, wdn_ref, gf_ref, out_ref, *, final_norm):
    h1 = h_ref[...] + _dot(o_ref[...], wout_ref[...])
    hn = (_rms_scale(h1) * g_ref[...]).astype(BF16)
    out_ref[...] = h1
    d_ff = wup_ref.shape[1]
    tf = min(COL_TILE, d_ff)
    for c in range(d_ff // tf):
        cols = slice(c * tf, (c + 1) * tf)
        act = jnp.square(jnp.maximum(_dot(hn, wup_ref[:, cols]), 0.0)).astype(BF16)
        out_ref[...] += _dot(act, wdn_ref[cols, :])
    if final_norm:
        out_ref[...] = _rms_scale(out_ref[...]) * gf_ref[...]


def _out_mlp(o, h, w_out, g, w_up, w_down, g_final, final_norm):
    n, d = h.shape
    tm = min(ROW_TILE, n)
    row = lambda i: (i, 0)
    return pl.pallas_call(
        functools.partial(_out_mlp_kernel, final_norm=final_norm),
        grid=(n // tm,),
        in_specs=[pl.BlockSpec((tm, o.shape[1]), row), pl.BlockSpec((tm, d), row),
                  _const_spec(w_out.shape), _const_spec((1, d)),
                  _const_spec(w_up.shape), _const_spec(w_down.shape), _const_spec((1, d))],
        out_specs=pl.BlockSpec((tm, d), row),
        out_shape=jax.ShapeDtypeStruct((n, d), F32),
        compiler_params=pltpu.CompilerParams(
            dimension_semantics=("parallel",), vmem_limit_bytes=VMEM_LIMIT_BYTES),
        name="out_mlp",
    )(o, h, w_out, g, w_up, w_down, g_final)


def _unit_lower_inverse(a_low, row, col):
    c = a_low.shape[0]
    eye = jnp.where(row == col, 1.0, 0.0).astype(F32)
    s = 1
    m = None
    while s < c:
        same_pair = (row // (2 * s)) == (col // (2 * s))
        off = same_pair & ((row % (2 * s)) >= s) & ((col % (2 * s)) < s)
        x = jnp.where(off, a_low, 0.0)
        if m is None:
            m = eye - x
        else:
            mb = m.astype(BF16)
            m = m - _dot(_dot(mb, x.astype(BF16)).astype(BF16), mb)
        s *= 2
    return m


def _gdn_kernel(q_ref, k_ref, v_ref, z_ref, ba_ref, cw_ref, alog_ref, dtb_ref, ong_ref,
                o_ref, xbuf_ref, state_ref, *, n_heads, chunk):
    t_idx = pl.program_id(1)
    tt = q_ref.shape[0]
    dh = q_ref.shape[1] // n_heads
    halo = DN_CONV - 1
    base = SUBLANES

    @pl.when(t_idx == 0)
    def _():
        state_ref[...] = jnp.zeros_like(state_ref)
        xbuf_ref[:, base - halo:base, :] = jnp.zeros((3, halo, xbuf_ref.shape[2]), F32)

    @pl.when(t_idx > 0)
    def _():
        xbuf_ref[:, base - halo:base, :] = xbuf_ref[:, base + tt - halo:base + tt, :]

    for s, ref in enumerate((q_ref, k_ref, v_ref)):
        xbuf_ref[s, base:base + tt, :] = ref[...]

    ba = ba_ref[...]
    lane = lax.broadcasted_iota(jnp.int32, ba.shape, 1)
    trow = lax.broadcasted_iota(jnp.int32, ba.shape, 0)
    beta_all = jax.nn.sigmoid(ba)
    sp_in = ba + dtb_ref[...]
    softplus = jnp.maximum(sp_in, 0.0) + jnp.log1p(jnp.exp(-jnp.abs(sp_in)))
    g_all = jnp.where((lane >= n_heads) & (lane < 2 * n_heads), -jnp.exp(alog_ref[...]) * softplus, 0.0)
    gc_all = g_all
    s = 1
    while s < chunk:
        shifted = pltpu.roll(gc_all, s, axis=0)
        gc_all = gc_all + jnp.where((trow % chunk) >= s, shifted, 0.0)
        s *= 2
    gc_all_t = gc_all.T

    row = lax.broadcasted_iota(jnp.int32, (chunk, chunk), 0)
    col = lax.broadcasted_iota(jnp.int32, (chunk, chunk), 1)

    def conv_silu(sec, rows0, lanes):
        acc = None
        for j in range(DN_CONV):
            xs = xbuf_ref[sec, pl.ds(base - halo + j + rows0, chunk), lanes]
            term = xs * cw_ref[j:j + 1, sec * n_heads * dh + lanes.start: sec * n_heads * dh + lanes.stop]
            acc = term if acc is None else acc + term
        return acc * jax.nn.sigmoid(acc)

    for c in range(tt // chunk):
        rows0 = c * chunk
        rows = slice(rows0, rows0 + chunk)
        for h in range(n_heads):
            lanes = slice(h * dh, (h + 1) * dh)
            q = conv_silu(0, rows0, lanes)
            k = conv_silu(1, rows0, lanes)
            v = conv_silu(2, rows0, lanes)
            q = q * (lax.rsqrt(jnp.sum(q * q, axis=-1, keepdims=True) + L2_EPS) * dh ** -0.5)
            k = k * lax.rsqrt(jnp.sum(k * k, axis=-1, keepdims=True) + L2_EPS)

            beta = beta_all[rows, h:h + 1]
            gc = gc_all[rows, n_heads + h:n_heads + h + 1]
            gc_row = gc_all_t[n_heads + h:n_heads + h + 1, rows]
            g_last = gc_all[rows0 + chunk - 1:rows0 + chunk, n_heads + h:n_heads + h + 1]

            decay = jnp.exp(jnp.where(row >= col, gc - gc_row, -jnp.inf))
            k_beta = k * beta
            kb16, k16, q16 = k_beta.astype(BF16), k.astype(BF16), q.astype(BF16)
            a_low = jnp.where(row > col, _dot_nt(kb16, k16) * decay, 0.0)
            t_inv = _unit_lower_inverse(a_low, row, col)
            e_gc = jnp.exp(gc)
            rhs = jnp.concatenate([v * beta, k_beta * e_gc], axis=-1).astype(BF16)
            sol = _dot(t_inv.astype(BF16), rhs)
            u, w = sol[:, :dh], sol[:, dh:]
            attn = _dot_nt(q16, k16) * decay
            q_dec = (q * e_gc).astype(BF16)
            k_dec = (k * jnp.exp(g_last - gc)).astype(BF16)

            state = state_ref[h]
            s16 = state.astype(BF16)
            v_new = u - _dot(w.astype(BF16), s16)
            vn16 = v_new.astype(BF16)
            o = _dot(q_dec, s16) + _dot(attn.astype(BF16), vn16)
            state_ref[h] = state * jnp.exp(g_last) + _dot_tn(k_dec, vn16)

            z = z_ref[rows, lanes]
            o = _rms_scale(o) * ong_ref[...] * (z * jax.nn.sigmoid(z))
            o_ref[rows, lanes] = o.astype(o_ref.dtype)


def _gdn_core(proj, ba, conv_w, a_log_row, dt_bias_row, out_norm_g, batch, seq):
    n_heads, dh = DN_HEADS, DN_HEAD_DIM
    width = n_heads * dh
    tt = min(GDN_TIME_TILE, seq)
    nt = seq // tt
    sec = lambda s: (lambda b, t: (b * nt + t, s))
    return pl.pallas_call(
        functools.partial(_gdn_kernel, n_heads=n_heads, chunk=min(GDN_CHUNK, tt)),
        grid=(batch, nt),
        in_specs=[pl.BlockSpec((tt, width), sec(0)), pl.BlockSpec((tt, width), sec(1)),
                  pl.BlockSpec((tt, width), sec(2)), pl.BlockSpec((tt, width), sec(3)),
                  pl.BlockSpec((tt, LANES), sec(0)),
                  _const_spec(conv_w.shape), _const_spec((1, LANES)), _const_spec((1, LANES)),
                  _const_spec((1, dh))],
        out_specs=pl.BlockSpec((tt, width), sec(0)),
        out_shape=jax.ShapeDtypeStruct((batch * seq, width), BF16),
        scratch_shapes=[pltpu.VMEM((3, tt + SUBLANES, width), F32),
                        pltpu.VMEM((n_heads, dh, dh), F32)],
        compiler_params=pltpu.CompilerParams(
            dimension_semantics=("parallel", "arbitrary"), vmem_limit_bytes=VMEM_LIMIT_BYTES),
        name="gdn_core",
    )(proj, proj, proj, proj, ba, conv_w, a_log_row, dt_bias_row, out_norm_g)


def _moba_kernel(q_ref, k_ref, v_ref, o_ref, kmean_ref, vt_ref, maskb_ref, *, n_heads, top_k):
    h = pl.program_id(1)
    i = pl.program_id(2)
    blk = q_ref.shape[0]
    seq, dh = k_ref.shape
    nb = seq // blk
    scale = dh ** -0.5

    @pl.when(i == 0)
    def _():
        r = lax.broadcasted_iota(jnp.int32, (nb, seq), 0)
        c = lax.broadcasted_iota(jnp.int32, (nb, seq), 1)
        ind = jnp.where(c // blk == r, 1.0, 0.0).astype(BF16)
        kmean_ref[...] = (_dot(ind, k_ref[...]) * (1.0 / blk)).astype(BF16)
        for j in range(nb):
            vt_ref[j] = v_ref[j * blk:(j + 1) * blk, :].astype(F32).T.astype(BF16)

    q = q_ref[...]

    gate = _dot_nt(kmean_ref[...], q)
    bidx = lax.broadcasted_iota(jnp.int32, gate.shape, 0)
    past = bidx < i
    gate = jnp.where(past, gate, -jnp.inf)
    rank = jnp.zeros(gate.shape, F32)
    for jp in range(nb):
        other = gate[jp:jp + 1, :]
        beats = (other > gate) | ((other == gate) & (jp < bidx))
        rank = rank + jnp.where(beats, 1.0, 0.0)
    maskb_ref[...] = jnp.where(past & (rank < top_k), 0.0, -jnp.inf)

    slope = jnp.exp2(jnp.zeros((1, 1), F32) - 8.0 * (h + 1).astype(F32) / n_heads)
    krow = lax.broadcasted_iota(jnp.int32, (blk, blk), 0)
    qcol = lax.broadcasted_iota(jnp.int32, (blk, blk), 1)
    key_bias = slope * krow.astype(F32)

    k_own = k_ref[pl.ds(pl.multiple_of(i * blk, blk), blk), :]
    s = _dot_nt(k_own, q) * scale + key_bias
    s = jnp.where(krow <= qcol, s, -jnp.inf)
    m = jnp.max(s, axis=0, keepdims=True)
    p = jnp.exp(s - m)
    l = jnp.sum(p, axis=0, keepdims=True)
    acc = _dot(vt_ref[i], p.astype(BF16))

    def past_block(j, carry):
        m, l, acc = carry
        k_j = k_ref[pl.ds(pl.multiple_of(j * blk, blk), blk), :]
        row_bias = maskb_ref[pl.ds(j, 1), :] + slope * ((j - i) * blk).astype(F32)
        t = _dot_nt(k_j, q) * scale + key_bias
        m_new = jnp.maximum(m, jnp.max(t, axis=0, keepdims=True) + row_bias)
        alpha = jnp.exp(m - m_new)
        p = jnp.exp(t + (row_bias - m_new))
        l = l * alpha + jnp.sum(p, axis=0, keepdims=True)
        acc = acc * alpha + _dot(vt_ref[j], p.astype(BF16))
        return m_new, l, acc

    m, l, acc = lax.fori_loop(0, i, past_block, (m, l, acc))
    o_ref[...] = (acc / l).T.astype(o_ref.dtype)


def _moba_attn(qkv, batch, seq):
    n_heads = MB_HEADS
    dh = qkv.shape[1] // (3 * n_heads)
    blk = min(MB_BLOCK, seq)
    nb = seq // blk
    return pl.pallas_call(
        functools.partial(_moba_kernel, n_heads=n_heads, top_k=min(MB_TOPK, nb)),
        grid=(batch, n_heads, nb),
        in_specs=[pl.BlockSpec((blk, dh), lambda b, h, i: (b * nb + i, h)),
                  pl.BlockSpec((seq, dh), lambda b, h, i: (b, n_heads + h)),
                  pl.BlockSpec((seq, dh), lambda b, h, i: (b, 2 * n_heads + h))],
        out_specs=pl.BlockSpec((blk, dh), lambda b, h, i: (b * nb + i, h)),
        out_shape=jax.ShapeDtypeStruct((batch * seq, n_heads * dh), BF16),
        scratch_shapes=[pltpu.VMEM((nb, dh), BF16),
                        pltpu.VMEM((nb, dh, blk), BF16),
                        pltpu.VMEM((nb, blk), F32)],
        compiler_params=pltpu.CompilerParams(
            dimension_semantics=("parallel", "parallel", "arbitrary"),
            vmem_limit_bytes=VMEM_LIMIT_BYTES),
        name="moba_attn",
    )(qkv, qkv, qkv)


def _pad_lanes(row_vec, offset):
    return jnp.zeros((1, LANES), F32).at[0, offset:offset + row_vec.shape[0]].set(row_vec.astype(F32))


def kernel(x, mix_norm_g, mlp_norm_g, dn_w_in, dn_conv_w, dn_a_log, dn_dt_bias, dn_out_norm_g, dn_w_out,
           mb_w_in, mb_w_out, mlp_w_up, mlp_w_down, final_norm_g):
    batch, seq, d = x.shape
    n = batch * seq
    h = x.reshape(n, d)
    dn_dim = DN_HEADS * DN_HEAD_DIM
    row = lambda v: v.reshape(1, -1).astype(F32)

    w_main = dn_w_in[0, :, :4 * dn_dim].astype(BF16)
    w_ba = jnp.zeros((d, LANES), F32).at[:, :2 * DN_HEADS].set(dn_w_in[0, :, 4 * dn_dim:]).astype(BF16)
    proj, ba = _rms_proj(h, row(mix_norm_g[0]), [w_main, w_ba], [F32, F32])
    o = _gdn_core(proj, ba, dn_conv_w[0].astype(F32), _pad_lanes(dn_a_log[0], DN_HEADS),
                  _pad_lanes(dn_dt_bias[0], DN_HEADS), row(dn_out_norm_g[0]), batch, seq)
    h = _out_mlp(o, h, dn_w_out[0].astype(BF16), row(mlp_norm_g[0]), mlp_w_up[0].astype(BF16),
                 mlp_w_down[0].astype(BF16), row(final_norm_g), final_norm=False)

    (qkv,) = _rms_proj(h, row(mix_norm_g[1]), [mb_w_in[0].astype(BF16)], [BF16])
    o = _moba_attn(qkv, batch, seq)
    h = _out_mlp(o, h, mb_w_out[0].astype(BF16), row(mlp_norm_g[1]), mlp_w_up[1].astype(BF16),
                 mlp_w_down[1].astype(BF16), row(final_norm_g), final_norm=True)
    return h.reshape(batch, seq, d)
```

```python
import functools
import math

import jax
import jax.numpy as jnp
from jax import lax
from jax.experimental import pallas as pl
from jax.experimental.pallas import tpu as pltpu

F32 = jnp.float32
BF16 = jnp.bfloat16

RMS_EPS = 1e-6
L2_EPS = 1e-6

DN_HEADS = 8
DN_HEAD_DIM = 128
DN_CONV = 4
GDN_CHUNK = 128
MB_HEADS = 8
MB_BLOCK = 256
MB_TOPK = 3

LANES = 128
SUBLANES = 8
VMEM_LIMIT_BYTES = 56 * 1024 * 1024

ROW_TILE = 512
COL_TILE = 512
GDN_TIME_TILE = 256


def _dot(a, b):
    return jnp.dot(a, b, preferred_element_type=F32)


def _dot_nt(a, b):
    return lax.dot_general(a, b, (((1,), (1,)), ((), ())), preferred_element_type=F32)


def _dot_tn(a, b):
    return lax.dot_general(a, b, (((0,), (0,)), ((), ())), preferred_element_type=F32)


def _rms_scale(x):
    return x * lax.rsqrt(jnp.mean(x * x, axis=-1, keepdims=True) + RMS_EPS)


def _const_spec(shape):
    return pl.BlockSpec(shape, lambda *_: (0,) * len(shape), pipeline_mode=pl.Buffered(1))


def _rms_proj_kernel(x_ref, g_ref, *refs, n_w):
    w_refs, o_refs = refs[:n_w], refs[n_w:]
    hn = (_rms_scale(x_ref[...]) * g_ref[...]).astype(BF16)
    for w_ref, o_ref in zip(w_refs, o_refs):
        n_cols = w_ref.shape[1]
        tn = min(COL_TILE, n_cols)
        for c in range(n_cols // tn):
            cols = slice(c * tn, (c + 1) * tn)
            o_ref[:, cols] = _dot(hn, w_ref[:, cols]).astype(o_ref.dtype)


def _rms_proj(x, g, ws, out_dtypes):
    n, d = x.shape
    tm = min(ROW_TILE, n)
    in_specs = [pl.BlockSpec((tm, d), lambda i: (i, 0)), _const_spec((1, d))]
    in_specs += [_const_spec(w.shape) for w in ws]
    out_specs = [pl.BlockSpec((tm, w.shape[1]), lambda i: (i, 0)) for w in ws]
    out_shape = [jax.ShapeDtypeStruct((n, w.shape[1]), dt) for w, dt in zip(ws, out_dtypes)]
    return pl.pallas_call(
        functools.partial(_rms_proj_kernel, n_w=len(ws)),
        grid=(n // tm,),
        in_specs=in_specs,
        out_specs=out_specs,
        out_shape=out_shape,
        compiler_params=pltpu.CompilerParams(
            dimension_semantics=("parallel",), vmem_limit_bytes=VMEM_LIMIT_BYTES),
        name="rms_proj",
    )(x, g, *ws)


def _out_mlp_kernel(o_ref, h_ref, wout_ref, g_ref, wup_ref, wdn_ref, gf_ref, out_ref, *, final_norm):
    h1 = h_ref[...] + _dot(o_ref[...], wout_ref[...])
    hn = (_rms_scale(h1) * g_ref[...]).astype(BF16)
    out_ref[...] = h1
    d_ff = wup_ref.shape[1]
    tf = min(COL_TILE, d_ff)
    for c in range(d_ff // tf):
        cols = slice(c * tf, (c + 1) * tf)
        act = jnp.square(jnp.maximum(_dot(hn, wup_ref[:, cols]), 0.0)).astype(BF16)
        out_ref[...] += _dot(act, wdn_ref[cols, :])
    if final_norm:
        out_ref[...] = _rms_scale(out_ref[...]) * gf_ref[...]


def _out_mlp(o, h, w_out, g, w_up, w_down, g_final, final_norm):
    n, d = h.shape
    tm = min(ROW_TILE, n)
    row = lambda i: (i, 0)
    return pl.pallas_call(
        functools.partial(_out_mlp_kernel, final_norm=final_norm),
        grid=(n // tm,),
        in_specs=[pl.BlockSpec((tm, o.shape[1]), row), pl.BlockSpec((tm, d), row),
                  _const_spec(w_out.shape), _const_spec((1, d)),
                  _const_spec(w_up.shape), _const_spec(w_down.shape), _const_spec((1, d))],
        out_specs=pl.BlockSpec((tm, d), row),
        out_shape=jax.ShapeDtypeStruct((n, d), F32),
        compiler_params=pltpu.CompilerParams(
            dimension_semantics=("parallel",), vmem_limit_bytes=VMEM_LIMIT_BYTES),
        name="out_mlp",
    )(o, h, w_out, g, w_up, w_down, g_final)


def _unit_lower_inverse_all(a_lows, row, col):
    c = a_lows[0].shape[0]
    eye = jnp.where(row == col, 1.0, 0.0).astype(F32)
    ms = None
    s = 1
    while s < c:
        same_pair = (row // (2 * s)) == (col // (2 * s))
        off = same_pair & ((row % (2 * s)) >= s) & ((col % (2 * s)) < s)
        xs = [jnp.where(off, a, 0.0) for a in a_lows]
        if ms is None:
            ms = [eye - x for x in xs]
        else:
            mbs = [m.astype(BF16) for m in ms]
            ys = [_dot(mb, x.astype(BF16)).astype(BF16) for mb, x in zip(mbs, xs)]
            ms = [m - _dot(y, mb) for m, y, mb in zip(ms, ys, mbs)]
        s *= 2
    return ms


def _gdn_kernel(q_ref, k_ref, v_ref, z_ref, ba_ref, cw_ref, alog_ref, dtb_ref, ong_ref,
                o_ref, xbuf_ref, state_ref, *, n_heads, chunk):
    t_idx = pl.program_id(1)
    tt = q_ref.shape[0]
    dh = q_ref.shape[1] // n_heads
    n_chunks = tt // chunk
    halo = DN_CONV - 1
    base = SUBLANES

    @pl.when(t_idx == 0)
    def _():
        state_ref[...] = jnp.zeros_like(state_ref)
        xbuf_ref[:, base - halo:base, :] = jnp.zeros((3, halo, xbuf_ref.shape[2]), F32)

    @pl.when(t_idx > 0)
    def _():
        xbuf_ref[:, base - halo:base, :] = xbuf_ref[:, base + tt - halo:base + tt, :]

    for s, ref in enumerate((q_ref, k_ref, v_ref)):
        xbuf_ref[s, base:base + tt, :] = ref[...]

    ba = ba_ref[...]
    lane = lax.broadcasted_iota(jnp.int32, ba.shape, 1)
    trow = lax.broadcasted_iota(jnp.int32, ba.shape, 0)
    beta_all = jax.nn.sigmoid(ba)
    sp_in = ba + dtb_ref[...]
    softplus = jnp.maximum(sp_in, 0.0) + jnp.log1p(jnp.exp(-jnp.abs(sp_in)))
    g_all = jnp.where((lane >= n_heads) & (lane < 2 * n_heads), -jnp.exp(alog_ref[...]) * softplus, 0.0)
    gc_all = g_all
    s = 1
    while s < chunk:
        shifted = pltpu.roll(gc_all, s, axis=0)
        gc_all = gc_all + jnp.where((trow % chunk) >= s, shifted, 0.0)
        s *= 2
    gc_all_t = gc_all.T
    g_last_all = jnp.concatenate(
        [jnp.broadcast_to(gc_all[(c + 1) * chunk - 1:(c + 1) * chunk, :], (chunk, LANES)) for c in range(n_chunks)],
        axis=0)
    e_gc_all = jnp.exp(gc_all)
    e_rem_all = jnp.exp(g_last_all - gc_all)
    e_last_all = jnp.exp(g_last_all)

    row = lax.broadcasted_iota(jnp.int32, (chunk, chunk), 0)
    col = lax.broadcasted_iota(jnp.int32, (chunk, chunk), 1)

    def conv_silu(sec, lanes):
        acc = None
        for j in range(DN_CONV):
            xs = xbuf_ref[sec, base - halo + j:base - halo + j + tt, lanes]
            term = xs * cw_ref[j:j + 1, sec * n_heads * dh + lanes.start: sec * n_heads * dh + lanes.stop]
            acc = term if acc is None else acc + term
        return acc * jax.nn.sigmoid(acc)

    heads = []
    for h in range(n_heads):
        lanes = slice(h * dh, (h + 1) * dh)
        q = conv_silu(0, lanes)
        k = conv_silu(1, lanes)
        v = conv_silu(2, lanes)
        q = q * (lax.rsqrt(jnp.sum(q * q, axis=-1, keepdims=True) + L2_EPS) * dh ** -0.5)
        k = k * lax.rsqrt(jnp.sum(k * k, axis=-1, keepdims=True) + L2_EPS)
        gl = n_heads + h
        beta = beta_all[:, h:h + 1]
        e_gc = e_gc_all[:, gl:gl + 1]
        k_beta = k * beta
        heads.append(dict(
            k16=k.astype(BF16), kb16=k_beta.astype(BF16), q16=q.astype(BF16),
            rhs16=jnp.concatenate([v * beta, k_beta * e_gc], axis=-1).astype(BF16),
            qd16=(q * e_gc).astype(BF16), kd16=(k * e_rem_all[:, gl:gl + 1]).astype(BF16)))

    pairs = [(c, h) for c in range(n_chunks) for h in range(n_heads)]
    rows_of = lambda c: slice(c * chunk, (c + 1) * chunk)
    kk = [_dot_nt(heads[h]["kb16"][rows_of(c)], heads[h]["k16"][rows_of(c)]) for c, h in pairs]
    qk = [_dot_nt(heads[h]["q16"][rows_of(c)], heads[h]["k16"][rows_of(c)]) for c, h in pairs]
    decay = []
    for c, h in pairs:
        gl = n_heads + h
        gc = gc_all[rows_of(c), gl:gl + 1]
        gc_row = gc_all_t[gl:gl + 1, rows_of(c)]
        decay.append(jnp.exp(jnp.where(row >= col, gc - gc_row, -jnp.inf)))
    a_low = [jnp.where(row > col, x * d, 0.0) for x, d in zip(kk, decay)]
    attn16 = [(x * d).astype(BF16) for x, d in zip(qk, decay)]
    t_inv = _unit_lower_inverse_all(a_low, row, col)
    sol = [_dot(t.astype(BF16), heads[h]["rhs16"][rows_of(c)]) for t, (c, h) in zip(t_inv, pairs)]

    for c in range(n_chunks):
        rows = rows_of(c)
        idx = [c * n_heads + h for h in range(n_heads)]
        states = [state_ref[h] for h in range(n_heads)]
        s16 = [s.astype(BF16) for s in states]
        ws = [_dot(sol[i][:, dh:].astype(BF16), s16[h]) for h, i in enumerate(idx)]
        vn16 = [(sol[i][:, :dh] - w).astype(BF16) for i, w in zip(idx, ws)]
        o_inter = [_dot(heads[h]["qd16"][rows], s16[h]) for h in range(n_heads)]
        o_intra = [_dot(attn16[i], vn16[h]) for h, i in enumerate(idx)]
        s_upd = [_dot_tn(heads[h]["kd16"][rows], vn16[h]) for h in range(n_heads)]
        for h in range(n_heads):
            gl = n_heads + h
            lanes = slice(h * dh, (h + 1) * dh)
            state_ref[h] = states[h] * e_last_all[c * chunk:c * chunk + 1, gl:gl + 1] + s_upd[h]
            z = z_ref[rows, lanes]
            o = _rms_scale(o_inter[h] + o_intra[h]) * ong_ref[...] * (z * jax.nn.sigmoid(z))
            o_ref[rows, lanes] = o.astype(o_ref.dtype)


def _gdn_core(proj, ba, conv_w, a_log_row, dt_bias_row, out_norm_g, batch, seq):
    n_heads, dh = DN_HEADS, DN_HEAD_DIM
    width = n_heads * dh
    tt = min(GDN_TIME_TILE, seq)
    nt = seq // tt
    sec = lambda s: (lambda b, t: (b * nt + t, s))
    return pl.pallas_call(
        functools.partial(_gdn_kernel, n_heads=n_heads, chunk=min(GDN_CHUNK, tt)),
        grid=(batch, nt),
        in_specs=[pl.BlockSpec((tt, width), sec(0)), pl.BlockSpec((tt, width), sec(1)),
                  pl.BlockSpec((tt, width), sec(2)), pl.BlockSpec((tt, width), sec(3)),
                  pl.BlockSpec((tt, LANES), sec(0)),
                  _const_spec(conv_w.shape), _const_spec((1, LANES)), _const_spec((1, LANES)),
                  _const_spec((1, dh))],
        out_specs=pl.BlockSpec((tt, width), sec(0)),
        out_shape=jax.ShapeDtypeStruct((batch * seq, width), BF16),
        scratch_shapes=[pltpu.VMEM((3, tt + SUBLANES, width), F32),
                        pltpu.VMEM((n_heads, dh, dh), F32)],
        compiler_params=pltpu.CompilerParams(
            dimension_semantics=("parallel", "arbitrary"), vmem_limit_bytes=VMEM_LIMIT_BYTES),
        name="gdn_core",
    )(proj, proj, proj, proj, ba, conv_w, a_log_row, dt_bias_row, out_norm_g)


def _moba_kernel(q_ref, k_ref, v_ref, o_ref, kmean_ref, vt_ref, maskb_ref, kbias_ref,
                 m_ref, l_ref, acc_ref, *, n_heads, top_k):
    i = pl.program_id(1)
    blk = q_ref.shape[0]
    seq = k_ref.shape[0]
    dh = k_ref.shape[1] // n_heads
    nb = seq // blk
    log2e = math.log2(math.e)
    scale2 = dh ** -0.5 * log2e
    heads = range(n_heads)
    lanes = [slice(h * dh, (h + 1) * dh) for h in heads]
    slope2 = [2.0 ** (-8.0 * (h + 1) / n_heads) * log2e for h in heads]

    krow = lax.broadcasted_iota(jnp.int32, (blk, blk), 0)
    qcol = lax.broadcasted_iota(jnp.int32, (blk, blk), 1)

    @pl.when(i == 0)
    def _():
        r = lax.broadcasted_iota(jnp.int32, (nb, seq), 0)
        c = lax.broadcasted_iota(jnp.int32, (nb, seq), 1)
        ind = jnp.where(c // blk == r, 1.0, 0.0).astype(BF16)
        for h in heads:
            kmean_ref[h] = (_dot(ind, k_ref[:, lanes[h]]) * (1.0 / blk)).astype(BF16)
            kbias_ref[h] = slope2[h] * krow.astype(F32)
            for j in range(nb):
                vt_ref[h, j] = v_ref[j * blk:(j + 1) * blk, lanes[h]].astype(F32).T.astype(BF16)

    qs = [q_ref[:, lanes[h]] for h in heads]

    gates = [_dot_nt(kmean_ref[h], qs[h]) for h in heads]
    bidx = lax.broadcasted_iota(jnp.int32, (nb, blk), 0)
    past = bidx < i
    for h in heads:
        gate = jnp.where(past, gates[h], -jnp.inf)
        rank = jnp.zeros(gate.shape, F32)
        for jp in range(nb):
            other = gate[jp:jp + 1, :]
            beats = (other > gate) | ((other == gate) & (jp < bidx))
            rank = rank + jnp.where(beats, 1.0, 0.0)
        maskb_ref[h] = jnp.where(past & (rank < top_k), 0.0, -jnp.inf)

    own = pl.ds(pl.multiple_of(i * blk, blk), blk)
    raw = [_dot_nt(k_ref[own, lanes[h]], qs[h]) for h in heads]
    ps = []
    for h in heads:
        s = jnp.where(krow <= qcol, raw[h] * scale2 + kbias_ref[h], -jnp.inf)
        m = jnp.max(s, axis=0, keepdims=True)
        p = jnp.exp2(s - m)
        m_ref[h] = m
        l_ref[h] = jnp.sum(p, axis=0, keepdims=True)
        ps.append(p.astype(BF16))
    for h in heads:
        acc_ref[h] = _dot(vt_ref[h, i], ps[h])

    def past_block(j, carry):
        rows = pl.ds(pl.multiple_of(j * blk, blk), blk)
        raw = [_dot_nt(k_ref[rows, lanes[h]], qs[h]) for h in heads]
        ps, alphas = [], []
        for h in heads:
            t = raw[h] * scale2 + kbias_ref[h]
            row_bias = maskb_ref[h, pl.ds(j, 1), :] + slope2[h] * ((j - i) * blk).astype(F32)
            m_old = m_ref[h]
            m_new = jnp.maximum(m_old, jnp.max(t, axis=0, keepdims=True) + row_bias)
            alpha = jnp.exp2(m_old - m_new)
            p = jnp.exp2(t + (row_bias - m_new))
            l_ref[h] = l_ref[h] * alpha + jnp.sum(p, axis=0, keepdims=True)
            m_ref[h] = m_new
            ps.append(p.astype(BF16))
            alphas.append(alpha)
        for h in heads:
            acc_ref[h] = acc_ref[h] * alphas[h] + _dot(vt_ref[h, j], ps[h])
        return carry

    lax.fori_loop(0, i, past_block, 0)
    for h in heads:
        o_ref[:, lanes[h]] = (acc_ref[h] / l_ref[h]).T.astype(o_ref.dtype)


def _moba_attn(qkv, batch, seq):
    n_heads = MB_HEADS
    width = qkv.shape[1] // 3
    dh = width // n_heads
    blk = min(MB_BLOCK, seq)
    nb = seq // blk
    return pl.pallas_call(
        functools.partial(_moba_kernel, n_heads=n_heads, top_k=min(MB_TOPK, nb)),
        grid=(batch, nb),
        in_specs=[pl.BlockSpec((blk, width), lambda b, i: (b * nb + i, 0)),
                  pl.BlockSpec((seq, width), lambda b, i: (b, 1)),
                  pl.BlockSpec((seq, width), lambda b, i: (b, 2))],
        out_specs=pl.BlockSpec((blk, width), lambda b, i: (b * nb + i, 0)),
        out_shape=jax.ShapeDtypeStruct((batch * seq, width), BF16),
        scratch_shapes=[pltpu.VMEM((n_heads, nb, dh), BF16),
                        pltpu.VMEM((n_heads, nb, dh, blk), BF16),
                        pltpu.VMEM((n_heads, nb, blk), F32),
                        pltpu.VMEM((n_heads, blk, blk), F32),
                        pltpu.VMEM((n_heads, 1, blk), F32),
                        pltpu.VMEM((n_heads, 1, blk), F32),
                        pltpu.VMEM((n_heads, dh, blk), F32)],
        compiler_params=pltpu.CompilerParams(
            dimension_semantics=("parallel", "arbitrary"), vmem_limit_bytes=VMEM_LIMIT_BYTES),
        name="moba_attn",
    )(qkv, qkv, qkv)


def _pad_lanes(row_vec, offset):
    return jnp.zeros((1, LANES), F32).at[0, offset:offset + row_vec.shape[0]].set(row_vec.astype(F32))


def kernel(x, mix_norm_g, mlp_norm_g, dn_w_in, dn_conv_w, dn_a_log, dn_dt_bias, dn_out_norm_g, dn_w_out,
           mb_w_in, mb_w_out, mlp_w_up, mlp_w_down, final_norm_g):
    batch, seq, d = x.shape
    n = batch * seq
    h = x.reshape(n, d)
    dn_dim = DN_HEADS * DN_HEAD_DIM
    row = lambda v: v.reshape(1, -1).astype(F32)

    w_main = dn_w_in[0, :, :4 * dn_dim].astype(BF16)
    w_ba = jnp.zeros((d, LANES), F32).at[:, :2 * DN_HEADS].set(dn_w_in[0, :, 4 * dn_dim:]).astype(BF16)
    proj, ba = _rms_proj(h, row(mix_norm_g[0]), [w_main, w_ba], [F32, F32])
    o = _gdn_core(proj, ba, dn_conv_w[0].astype(F32), _pad_lanes(dn_a_log[0], DN_HEADS),
                  _pad_lanes(dn_dt_bias[0], DN_HEADS), row(dn_out_norm_g[0]), batch, seq)
    h = _out_mlp(o, h, dn_w_out[0].astype(BF16), row(mlp_norm_g[0]), mlp_w_up[0].astype(BF16),
                 mlp_w_down[0].astype(BF16), row(final_norm_g), final_norm=False)

    (qkv,) = _rms_proj(h, row(mix_norm_g[1]), [mb_w_in[0].astype(BF16)], [BF16])
    o = _moba_attn(qkv, batch, seq)
    h = _out_mlp(o, h, mb_w_out[0].astype(BF16), row(mlp_norm_g[1]), mlp_w_up[1].astype(BF16),
                 mlp_w_down[1].astype(BF16), row(final_norm_g), final_norm=True)
    return h.reshape(batch, seq, d)
```

```python
import functools
import math

import jax
import jax.numpy as jnp
from jax import lax
from jax.experimental import pallas as pl
from jax.experimental.pallas import tpu as pltpu

F32 = jnp.float32
BF16 = jnp.bfloat16

RMS_EPS = 1e-6
L2_EPS = 1e-6

DN_HEADS = 8
DN_HEAD_DIM = 128
DN_CONV = 4
GDN_CHUNK = 128
MB_HEADS = 8
MB_BLOCK = 256
MB_TOPK = 3

LANES = 128
SUBLANES = 8
VMEM_LIMIT_BYTES = 56 * 1024 * 1024

ROW_TILE = 512
COL_TILE = 512
GDN_TIME_TILE = 256
MOBA_SUM_ROWS = 16
MOBA_SCORE_LEAD = 4


def _dot(a, b):
    return jnp.dot(a, b, preferred_element_type=F32)


def _dot_nt(a, b):
    return lax.dot_general(a, b, (((1,), (1,)), ((), ())), preferred_element_type=F32)


def _dot_tn(a, b):
    return lax.dot_general(a, b, (((0,), (0,)), ((), ())), preferred_element_type=F32)


def _rms_scale(x):
    return x * lax.rsqrt(jnp.mean(x * x, axis=-1, keepdims=True) + RMS_EPS)


def _const_spec(shape):
    return pl.BlockSpec(shape, lambda *_: (0,) * len(shape), pipeline_mode=pl.Buffered(1))


def _rms_proj_kernel(x_ref, g_ref, *refs, n_w):
    w_refs, o_refs = refs[:n_w], refs[n_w:]
    hn = (_rms_scale(x_ref[...]) * g_ref[...]).astype(BF16)
    for w_ref, o_ref in zip(w_refs, o_refs):
        n_cols = w_ref.shape[1]
        tn = min(COL_TILE, n_cols)
        for c in range(n_cols // tn):
            cols = slice(c * tn, (c + 1) * tn)
            o_ref[:, cols] = _dot(hn, w_ref[:, cols]).astype(o_ref.dtype)


def _rms_proj(x, g, ws, out_dtypes):
    n, d = x.shape
    tm = min(ROW_TILE, n)
    in_specs = [pl.BlockSpec((tm, d), lambda i: (i, 0)), _const_spec((1, d))]
    in_specs += [_const_spec(w.shape) for w in ws]
    out_specs = [pl.BlockSpec((tm, w.shape[1]), lambda i: (i, 0)) for w in ws]
    out_shape = [jax.ShapeDtypeStruct((n, w.shape[1]), dt) for w, dt in zip(ws, out_dtypes)]
    return pl.pallas_call(
        functools.partial(_rms_proj_kernel, n_w=len(ws)),
        grid=(n // tm,),
        in_specs=in_specs,
        out_specs=out_specs,
        out_shape=out_shape,
        compiler_params=pltpu.CompilerParams(
            dimension_semantics=("parallel",), vmem_limit_bytes=VMEM_LIMIT_BYTES),
        name="rms_proj",
    )(x, g, *ws)


def _out_mlp_kernel(o_ref, h_ref, wout_ref, g_ref, wup_ref, wdn_ref, gf_ref, out_ref, *, final_norm):
    h1 = h_ref[...] + _dot(o_ref[...], wout_ref[...])
    hn = (_rms_scale(h1) * g_ref[...]).astype(BF16)
    out_ref[...] = h1
    d_ff = wup_ref.shape[1]
    tf = min(COL_TILE, d_ff)
    for c in range(d_ff // tf):
        cols = slice(c * tf, (c + 1) * tf)
        act = jnp.square(jnp.maximum(_dot(hn, wup_ref[:, cols]), 0.0)).astype(BF16)
        out_ref[...] += _dot(act, wdn_ref[cols, :])
    if final_norm:
        out_ref[...] = _rms_scale(out_ref[...]) * gf_ref[...]


def _out_mlp(o, h, w_out, g, w_up, w_down, g_final, final_norm):
    n, d = h.shape
    tm = min(ROW_TILE, n)
    row = lambda i: (i, 0)
    return pl.pallas_call(
        functools.partial(_out_mlp_kernel, final_norm=final_norm),
        grid=(n // tm,),
        in_specs=[pl.BlockSpec((tm, o.shape[1]), row), pl.BlockSpec((tm, d), row),
                  _const_spec(w_out.shape), _const_spec((1, d)),
                  _const_spec(w_up.shape), _const_spec(w_down.shape), _const_spec((1, d))],
        out_specs=pl.BlockSpec((tm, d), row),
        out_shape=jax.ShapeDtypeStruct((n, d), F32),
        compiler_params=pltpu.CompilerParams(
            dimension_semantics=("parallel",), vmem_limit_bytes=VMEM_LIMIT_BYTES),
        name="out_mlp",
    )(o, h, w_out, g, w_up, w_down, g_final)


def _unit_lower_inverse_all(a_lows, row, col):
    c = a_lows[0].shape[0]
    eye = jnp.where(row == col, 1.0, 0.0).astype(F32)
    negs = [(-a).astype(BF16) for a in a_lows]
    ms = None
    s = 1
    while s < c:
        same_pair = (row // (2 * s)) == (col // (2 * s))
        off = same_pair & ((row % (2 * s)) >= s) & ((col % (2 * s)) < s)
        if ms is None:
            ms = [(eye - jnp.where(off, a, 0.0)).astype(BF16) for a in a_lows]
        else:
            ys = [_dot(m, n).astype(BF16) for m, n in zip(ms, negs)]
            ms = [jnp.where(off, _dot(y, m).astype(BF16), m) for y, m in zip(ys, ms)]
        s *= 2
    return ms


def _gdn_kernel(q_ref, k_ref, v_ref, z_ref, ba_ref, cw_ref, alog_ref, dtb_ref, ong_ref,
                o_ref, halo_ref, state_ref, *, n_heads, chunk):
    t_idx = pl.program_id(1)
    tt = q_ref.shape[0]
    dh = q_ref.shape[1] // n_heads
    n_chunks = tt // chunk

    @pl.when(t_idx == 0)
    def _():
        state_ref[...] = jnp.zeros_like(state_ref)
        halo_ref[...] = jnp.zeros_like(halo_ref)

    ba = ba_ref[...]
    lane = lax.broadcasted_iota(jnp.int32, ba.shape, 1)
    trow = lax.broadcasted_iota(jnp.int32, ba.shape, 0)
    beta_all = jax.nn.sigmoid(ba)
    sp_in = ba + dtb_ref[...]
    softplus = jnp.maximum(sp_in, 0.0) + jnp.log(1.0 + jnp.exp(-jnp.abs(sp_in)))
    g_all = jnp.where((lane >= n_heads) & (lane < 2 * n_heads), -jnp.exp(alog_ref[...]) * softplus, 0.0)
    gc_all = g_all
    s = 1
    while s < chunk:
        shifted = pltpu.roll(gc_all, s, axis=0)
        gc_all = gc_all + jnp.where((trow % chunk) >= s, shifted, 0.0)
        s *= 2
    gc_all_t = gc_all.T
    g_last_all = jnp.concatenate(
        [jnp.broadcast_to(gc_all[(c + 1) * chunk - 1:(c + 1) * chunk, :], (chunk, LANES)) for c in range(n_chunks)],
        axis=0)
    e_gc_all = jnp.exp(gc_all)
    e_rem_all = jnp.exp(g_last_all - gc_all)
    e_last_all = jnp.exp(g_last_all)

    row = lax.broadcasted_iota(jnp.int32, (chunk, chunk), 0)
    col = lax.broadcasted_iota(jnp.int32, (chunk, chunk), 1)

    def conv_silu(sec, lanes):
        x = (q_ref, k_ref, v_ref)[sec][:, lanes]
        groups = tt // SUBLANES
        xg = jnp.concatenate([x, halo_ref[sec, :, lanes]], axis=0).reshape(groups + 1, SUBLANES, dh)
        sub = lax.broadcasted_iota(jnp.int32, (groups, SUBLANES, dh), 1)
        w0 = sec * n_heads * dh + lanes.start
        acc = x * cw_ref[DN_CONV - 1:DN_CONV, w0:w0 + dh]
        for s in range(1, DN_CONV):
            r = pltpu.roll(xg, s, axis=1)
            prev = jnp.concatenate([r[groups:], r[:groups - 1]], axis=0)
            shifted = jnp.where(sub >= s, r[:groups], prev).reshape(tt, dh)
            j = DN_CONV - 1 - s
            acc = acc + shifted * cw_ref[j:j + 1, w0:w0 + dh]
        return acc * jax.nn.sigmoid(acc)

    heads = []
    for h in range(n_heads):
        lanes = slice(h * dh, (h + 1) * dh)
        q = conv_silu(0, lanes)
        k = conv_silu(1, lanes)
        v = conv_silu(2, lanes)
        q = q * (lax.rsqrt(jnp.sum(q * q, axis=-1, keepdims=True) + L2_EPS) * dh ** -0.5)
        k = k * lax.rsqrt(jnp.sum(k * k, axis=-1, keepdims=True) + L2_EPS)
        gl = n_heads + h
        beta = beta_all[:, h:h + 1]
        e_gc = e_gc_all[:, gl:gl + 1]
        k_beta = k * beta
        heads.append(dict(
            k16=k.astype(BF16), kb16=k_beta.astype(BF16), q16=q.astype(BF16),
            rhs16=jnp.concatenate([v * beta, k_beta * e_gc], axis=-1).astype(BF16),
            qd16=(q * e_gc).astype(BF16), kd16=(k * e_rem_all[:, gl:gl + 1]).astype(BF16)))
    for sec, ref in enumerate((q_ref, k_ref, v_ref)):
        halo_ref[sec] = ref[tt - SUBLANES:tt, :]

    pairs = [(c, h) for c in range(n_chunks) for h in range(n_heads)]
    rows_of = lambda c: slice(c * chunk, (c + 1) * chunk)
    kk = [_dot_nt(heads[h]["kb16"][rows_of(c)], heads[h]["k16"][rows_of(c)]) for c, h in pairs]
    qk = [_dot_nt(heads[h]["q16"][rows_of(c)], heads[h]["k16"][rows_of(c)]) for c, h in pairs]
    decay = []
    for c, h in pairs:
        gl = n_heads + h
        gc = gc_all[rows_of(c), gl:gl + 1]
        gc_row = gc_all_t[gl:gl + 1, rows_of(c)]
        decay.append(jnp.exp(jnp.where(row >= col, gc - gc_row, -jnp.inf)))
    a_low = [jnp.where(row > col, x * d, 0.0) for x, d in zip(kk, decay)]
    attn16 = [(x * d).astype(BF16) for x, d in zip(qk, decay)]
    t_inv = _unit_lower_inverse_all(a_low, row, col)
    sol = [_dot(t.astype(BF16), heads[h]["rhs16"][rows_of(c)]) for t, (c, h) in zip(t_inv, pairs)]

    for c in range(n_chunks):
        rows = rows_of(c)
        idx = [c * n_heads + h for h in range(n_heads)]
        states = [state_ref[h] for h in range(n_heads)]
        s16 = [s.astype(BF16) for s in states]
        ws = [_dot(sol[i][:, dh:].astype(BF16), s16[h]) for h, i in enumerate(idx)]
        vn16 = [(sol[i][:, :dh] - w).astype(BF16) for i, w in zip(idx, ws)]
        o_inter = [_dot(heads[h]["qd16"][rows], s16[h]) for h in range(n_heads)]
        o_intra = [_dot(attn16[i], vn16[h]) for h, i in enumerate(idx)]
        s_upd = [_dot_tn(heads[h]["kd16"][rows], vn16[h]) for h in range(n_heads)]
        for h in range(n_heads):
            gl = n_heads + h
            lanes = slice(h * dh, (h + 1) * dh)
            state_ref[h] = states[h] * e_last_all[c * chunk:c * chunk + 1, gl:gl + 1] + s_upd[h]
            z = z_ref[rows, lanes]
            o = _rms_scale(o_inter[h] + o_intra[h]) * ong_ref[...] * (z * jax.nn.sigmoid(z))
            o_ref[rows, lanes] = o.astype(o_ref.dtype)


def _gdn_core(proj, ba, conv_w, a_log_row, dt_bias_row, out_norm_g, batch, seq):
    n_heads, dh = DN_HEADS, DN_HEAD_DIM
    width = n_heads * dh
    tt = min(GDN_TIME_TILE, seq)
    nt = seq // tt
    sec = lambda s: (lambda b, t: (b * nt + t, s))
    return pl.pallas_call(
        functools.partial(_gdn_kernel, n_heads=n_heads, chunk=min(GDN_CHUNK, tt)),
        grid=(batch, nt),
        in_specs=[pl.BlockSpec((tt, width), sec(0)), pl.BlockSpec((tt, width), sec(1)),
                  pl.BlockSpec((tt, width), sec(2)), pl.BlockSpec((tt, width), sec(3)),
                  pl.BlockSpec((tt, LANES), sec(0)),
                  _const_spec(conv_w.shape), _const_spec((1, LANES)), _const_spec((1, LANES)),
                  _const_spec((1, dh))],
        out_specs=pl.BlockSpec((tt, width), sec(0)),
        out_shape=jax.ShapeDtypeStruct((batch * seq, width), BF16),
        scratch_shapes=[pltpu.VMEM((3, SUBLANES, width), F32),
                        pltpu.VMEM((n_heads, dh, dh), F32)],
        compiler_params=pltpu.CompilerParams(
            dimension_semantics=("parallel", "arbitrary"), vmem_limit_bytes=VMEM_LIMIT_BYTES),
        name="gdn_core",
    )(proj, proj, proj, proj, ba, conv_w, a_log_row, dt_bias_row, out_norm_g)


def _split3_f32(x):
    hi = x.astype(BF16).astype(F32)
    mid = (x - hi).astype(BF16).astype(F32)
    return hi, mid, ((x - hi) - mid).astype(BF16).astype(F32)


def _moba_kernel(q_ref, k_ref, v_ref, o_ref, kmean_ref, vt_ref, maskb_ref, kaug_ref, qaug_ref,
                 m_ref, acc_ref, *, n_heads, top_k):
    i = pl.program_id(1)
    blk = q_ref.shape[0]
    seq = k_ref.shape[0]
    dh = k_ref.shape[1] // n_heads
    nb = seq // blk
    scale2 = dh ** -0.5 * math.log2(math.e)
    heads = range(n_heads)
    lanes = [slice(h * dh, (h + 1) * dh) for h in heads]
    slope_raw = [2.0 ** (-8.0 * (h + 1) / n_heads) * dh ** 0.5 for h in heads]

    krow = lax.broadcasted_iota(jnp.int32, (blk, blk), 0)
    qcol = lax.broadcasted_iota(jnp.int32, (blk, blk), 1)
    aug_lane = lax.broadcasted_iota(jnp.int32, (blk, dh), 1)

    @pl.when(i == 0)
    def _():
        r = lax.broadcasted_iota(jnp.int32, (nb, seq), 0)
        c = lax.broadcasted_iota(jnp.int32, (nb, seq), 1)
        ind = jnp.where(c // blk == r, 1.0, 0.0).astype(BF16)
        key_pos = lax.broadcasted_iota(jnp.int32, (blk, dh), 0).astype(F32)
        for h in heads:
            kmean_ref[h] = (_dot(ind, k_ref[:, lanes[h]]) * (1.0 / blk)).astype(BF16)
            hi, mid, lo = _split3_f32(slope_raw[h] * key_pos)
            kaug_ref[h] = jnp.where(aug_lane == 0, hi, jnp.where(aug_lane == 1, mid,
                                                                 jnp.where(aug_lane == 2, lo, 0.0))).astype(BF16)
            for j in range(nb):
                vt = v_ref[j * blk:(j + 1) * blk, lanes[h]].astype(F32).T.astype(BF16)
                vt_ref[h, j] = jnp.concatenate([vt, jnp.ones((MOBA_SUM_ROWS, blk), BF16)], axis=0)

    qs = [q_ref[:, lanes[h]] for h in heads]
    ones_cols = jnp.where(aug_lane < 3, 1.0, 0.0).astype(BF16)
    for h in heads:
        qaug_ref[h] = jnp.concatenate([qs[h], ones_cols], axis=1)

    gates = [_dot_nt(kmean_ref[h], qs[h]) for h in heads]
    bidx = lax.broadcasted_iota(jnp.int32, (nb, blk), 0)
    past = bidx < i
    for h in heads:
        gate = jnp.where(past, gates[h], -jnp.inf)
        rank = jnp.zeros(gate.shape, F32)
        for jp in range(nb):
            other = gate[jp:jp + 1, :]
            beats = (other > gate) | ((other == gate) & (jp < bidx))
            rank = rank + jnp.where(beats, 1.0, 0.0)
        maskb_ref[h] = jnp.where(past & (rank < top_k), 0.0, -jnp.inf)

    def pipelined(tiles, score_mm, rest):
        raw = {}
        for step in range(len(tiles) + MOBA_SCORE_LEAD):
            if step < len(tiles):
                raw[step] = score_mm(*tiles[step])
            if step >= MOBA_SCORE_LEAD:
                rest(*tiles[step - MOBA_SCORE_LEAD], raw.pop(step - MOBA_SCORE_LEAD))

    def scores(h, j):
        rows = pl.ds(pl.multiple_of(j * blk, blk), blk)
        k_aug = jnp.concatenate([k_ref[rows, lanes[h]], kaug_ref[h]], axis=1)
        return _dot_nt(k_aug, qaug_ref[h])

    def own_rest(h, j, raw):
        s = jnp.where(krow <= qcol, raw, -jnp.inf)
        m = jnp.max(s, axis=0, keepdims=True)
        m_ref[h] = m
        acc_ref[h] = _dot(vt_ref[h, j], jnp.exp2((s - m) * scale2).astype(BF16))

    pipelined([(h, i) for h in heads], scores, own_rest)

    def past_blocks(js):
        def rest(h, j, raw):
            row_bias = maskb_ref[h, pl.ds(j, 1), :] + slope_raw[h] * ((j - i) * blk).astype(F32)
            m_old = m_ref[h]
            m_new = jnp.maximum(m_old, jnp.max(raw, axis=0, keepdims=True) + row_bias)
            m_ref[h] = m_new
            p = jnp.exp2((raw + (row_bias - m_new)) * scale2).astype(BF16)
            acc_ref[h] = acc_ref[h] * jnp.exp2((m_old - m_new) * scale2) + _dot(vt_ref[h, j], p)

        pipelined([(h, j) for j in js for h in heads], scores, rest)

    def pair_body(jj, carry):
        past_blocks([2 * jj, 2 * jj + 1])
        return carry

    lax.fori_loop(0, i >> 1, pair_body, 0)

    @pl.when((i & 1) == 1)
    def _():
        past_blocks([i - 1])

    for h in heads:
        acc = acc_ref[h]
        o_ref[:, lanes[h]] = (acc[:dh] / acc[dh:dh + 1]).T.astype(o_ref.dtype)


def _moba_attn(qkv, batch, seq):
    n_heads = MB_HEADS
    width = qkv.shape[1] // 3
    dh = width // n_heads
    blk = min(MB_BLOCK, seq)
    nb = seq // blk
    return pl.pallas_call(
        functools.partial(_moba_kernel, n_heads=n_heads, top_k=min(MB_TOPK, nb)),
        grid=(batch, nb),
        in_specs=[pl.BlockSpec((blk, width), lambda b, i: (b * nb + i, 0)),
                  pl.BlockSpec((seq, width), lambda b, i: (b, 1)),
                  pl.BlockSpec((seq, width), lambda b, i: (b, 2))],
        out_specs=pl.BlockSpec((blk, width), lambda b, i: (b * nb + i, 0)),
        out_shape=jax.ShapeDtypeStruct((batch * seq, width), BF16),
        scratch_shapes=[pltpu.VMEM((n_heads, nb, dh), BF16),
                        pltpu.VMEM((n_heads, nb, dh + MOBA_SUM_ROWS, blk), BF16),
                        pltpu.VMEM((n_heads, nb, blk), F32),
                        pltpu.VMEM((n_heads, blk, dh), BF16),
                        pltpu.VMEM((n_heads, blk, 2 * dh), BF16),
                        pltpu.VMEM((n_heads, 1, blk), F32),
                        pltpu.VMEM((n_heads, dh + MOBA_SUM_ROWS, blk), F32)],
        compiler_params=pltpu.CompilerParams(
            dimension_semantics=("parallel", "arbitrary"), vmem_limit_bytes=VMEM_LIMIT_BYTES),
        name="moba_attn",
    )(qkv, qkv, qkv)


def _pad_lanes(row_vec, offset):
    return jnp.zeros((1, LANES), F32).at[0, offset:offset + row_vec.shape[0]].set(row_vec.astype(F32))


def kernel(x, mix_norm_g, mlp_norm_g, dn_w_in, dn_conv_w, dn_a_log, dn_dt_bias, dn_out_norm_g, dn_w_out,
           mb_w_in, mb_w_out, mlp_w_up, mlp_w_down, final_norm_g):
    batch, seq, d = x.shape
    n = batch * seq
    h = x.reshape(n, d)
    dn_dim = DN_HEADS * DN_HEAD_DIM
    row = lambda v: v.reshape(1, -1).astype(F32)

    w_main = dn_w_in[0, :, :4 * dn_dim].astype(BF16)
    w_ba = jnp.zeros((d, LANES), F32).at[:, :2 * DN_HEADS].set(dn_w_in[0, :, 4 * dn_dim:]).astype(BF16)
    proj, ba = _rms_proj(h, row(mix_norm_g[0]), [w_main, w_ba], [F32, F32])
    o = _gdn_core(proj, ba, dn_conv_w[0].astype(F32), _pad_lanes(dn_a_log[0], DN_HEADS),
                  _pad_lanes(dn_dt_bias[0], DN_HEADS), row(dn_out_norm_g[0]), batch, seq)
    h = _out_mlp(o, h, dn_w_out[0].astype(BF16), row(mlp_norm_g[0]), mlp_w_up[0].astype(BF16),
                 mlp_w_down[0].astype(BF16), row(final_norm_g), final_norm=False)

    (qkv,) = _rms_proj(h, row(mix_norm_g[1]), [mb_w_in[0].astype(BF16)], [BF16])
    o = _moba_attn(qkv, batch, seq)
    h = _out_mlp(o, h, mb_w_out[0].astype(BF16), row(mlp_norm_g[1]), mlp_w_up[1].astype(BF16),
                 mlp_w_down[1].astype(BF16), row(final_norm_g), final_norm=True)
    return h.reshape(batch, seq, d)
```

```python
import functools
import math

import jax
import jax.numpy as jnp
from jax import lax
from jax.experimental import pallas as pl
from jax.experimental.pallas import tpu as pltpu

F32 = jnp.float32
BF16 = jnp.bfloat16

RMS_EPS = 1e-6
L2_EPS = 1e-6

DN_HEADS = 8
DN_HEAD_DIM = 128
DN_CONV = 4
GDN_CHUNK = 128
MB_HEADS = 8
MB_BLOCK = 256
MB_TOPK = 3

LANES = 128
SUBLANES = 8
VMEM_LIMIT_BYTES = 56 * 1024 * 1024

ROW_TILE = 512
COL_TILE = 512
GDN_PROJ_COL_TILE = 256
GDN_TIME_TILE = 512
GDN_GROUP_CHUNKS = 2
MOBA_SUM_ROWS = 16
MOBA_SCORE_LEAD = 4


def _dot(a, b):
    return jnp.dot(a, b, preferred_element_type=F32)


def _dot_nt(a, b):
    return lax.dot_general(a, b, (((1,), (1,)), ((), ())), preferred_element_type=F32)


def _dot_tn(a, b):
    return lax.dot_general(a, b, (((0,), (0,)), ((), ())), preferred_element_type=F32)


def _rms_scale(x):
    return x * lax.rsqrt(jnp.mean(x * x, axis=-1, keepdims=True) + RMS_EPS)


def _const_spec(shape):
    return pl.BlockSpec(shape, lambda *_: (0,) * len(shape), pipeline_mode=pl.Buffered(1))


def _rms_proj_kernel(x_ref, g_ref, *refs, n_w):
    w_refs, o_refs = refs[:n_w], refs[n_w:]
    hn = (_rms_scale(x_ref[...]) * g_ref[...]).astype(BF16)
    for w_ref, o_ref in zip(w_refs, o_refs):
        n_cols = w_ref.shape[1]
        tn = min(COL_TILE, n_cols)
        for c in range(n_cols // tn):
            cols = slice(c * tn, (c + 1) * tn)
            o_ref[:, cols] = _dot(hn, w_ref[:, cols]).astype(o_ref.dtype)


def _rms_proj(x, g, ws, out_dtypes):
    n, d = x.shape
    tm = min(ROW_TILE, n)
    in_specs = [pl.BlockSpec((tm, d), lambda i: (i, 0)), _const_spec((1, d))]
    in_specs += [_const_spec(w.shape) for w in ws]
    out_specs = [pl.BlockSpec((tm, w.shape[1]), lambda i: (i, 0)) for w in ws]
    out_shape = [jax.ShapeDtypeStruct((n, w.shape[1]), dt) for w, dt in zip(ws, out_dtypes)]
    return pl.pallas_call(
        functools.partial(_rms_proj_kernel, n_w=len(ws)),
        grid=(n // tm,),
        in_specs=in_specs,
        out_specs=out_specs,
        out_shape=out_shape,
        compiler_params=pltpu.CompilerParams(
            dimension_semantics=("parallel",), vmem_limit_bytes=VMEM_LIMIT_BYTES),
        name="rms_proj",
    )(x, g, *ws)


def _gdn_proj_kernel(x_ref, g_ref, w_ref, wba_ref, cw_ref, q_ref, k_ref, v_ref, z_ref, ba_ref, halo_ref,
                     *, n_heads, tiles_per_seq):
    tm = x_ref.shape[0]
    width = q_ref.shape[1]
    dh = width // n_heads
    tc = min(GDN_PROJ_COL_TILE, width)
    groups = tm // SUBLANES
    hn = (_rms_scale(x_ref[...]) * g_ref[...]).astype(BF16)

    @pl.when(lax.rem(pl.program_id(0), tiles_per_seq) == 0)
    def _():
        halo_ref[...] = jnp.zeros_like(halo_ref)

    sub = lax.broadcasted_iota(jnp.int32, (groups, SUBLANES, tc), 1)
    for sec, o_ref in enumerate((q_ref, k_ref, v_ref)):
        for c in range(width // tc):
            cols = slice(sec * width + c * tc, sec * width + (c + 1) * tc)
            y = _dot(hn, w_ref[:, cols])
            yg = jnp.concatenate([y, halo_ref[:, cols]], axis=0).reshape(groups + 1, SUBLANES, tc)
            halo_ref[:, cols] = y[tm - SUBLANES:, :]
            acc = y * cw_ref[DN_CONV - 1:DN_CONV, cols]
            for s in range(1, DN_CONV):
                r = pltpu.roll(yg, s, axis=1)
                prev = jnp.concatenate([r[groups:], r[:groups - 1]], axis=0)
                shifted = jnp.where(sub >= s, r[:groups], prev).reshape(tm, tc)
                acc = acc + shifted * cw_ref[DN_CONV - 1 - s:DN_CONV - s, cols]
            a = acc * jax.nn.sigmoid(acc)
            if sec < 2:
                parts = []
                for hh in range(tc // dh):
                    p = a[:, hh * dh:(hh + 1) * dh]
                    inv = lax.rsqrt(jnp.sum(p * p, axis=-1, keepdims=True) + L2_EPS)
                    parts.append(p * (inv * dh ** -0.5 if sec == 0 else inv))
                a = jnp.concatenate(parts, axis=1)
            o_ref[:, c * tc:(c + 1) * tc] = a.astype(o_ref.dtype)
    for c in range(width // tc):
        z_ref[:, c * tc:(c + 1) * tc] = _dot(hn, w_ref[:, 3 * width + c * tc:3 * width + (c + 1) * tc])
    ba_ref[...] = _dot(hn, wba_ref[...])


def _gdn_proj(x, g, w_main, w_ba, conv_w, seq):
    n, d = x.shape
    width = w_main.shape[1] // 4
    tm = min(ROW_TILE, seq)
    row = lambda i: (i, 0)
    return pl.pallas_call(
        functools.partial(_gdn_proj_kernel, n_heads=DN_HEADS, tiles_per_seq=seq // tm),
        grid=(n // tm,),
        in_specs=[pl.BlockSpec((tm, d), row), _const_spec((1, d)), _const_spec(w_main.shape),
                  _const_spec(w_ba.shape), _const_spec(conv_w.shape)],
        out_specs=[pl.BlockSpec((tm, width), row)] * 4 + [pl.BlockSpec((tm, LANES), row)],
        out_shape=[jax.ShapeDtypeStruct((n, width), BF16)] * 3
                  + [jax.ShapeDtypeStruct((n, width), F32), jax.ShapeDtypeStruct((n, LANES), F32)],
        scratch_shapes=[pltpu.VMEM((SUBLANES, 3 * width), F32)],
        compiler_params=pltpu.CompilerParams(
            dimension_semantics=("arbitrary",), vmem_limit_bytes=VMEM_LIMIT_BYTES),
        name="gdn_proj",
    )(x, g, w_main, w_ba, conv_w)


def _out_mlp_kernel(o_ref, h_ref, wout_ref, g_ref, wup_ref, wdn_ref, gf_ref, out_ref, *, final_norm):
    h1 = h_ref[...] + _dot(o_ref[...], wout_ref[...])
    hn = (_rms_scale(h1) * g_ref[...]).astype(BF16)
    out_ref[...] = h1
    d_ff = wup_ref.shape[1]
    tf = min(COL_TILE, d_ff)
    for c in range(d_ff // tf):
        cols = slice(c * tf, (c + 1) * tf)
        act = jnp.square(jnp.maximum(_dot(hn, wup_ref[:, cols]), 0.0)).astype(BF16)
        out_ref[...] += _dot(act, wdn_ref[cols, :])
    if final_norm:
        out_ref[...] = _rms_scale(out_ref[...]) * gf_ref[...]


def _out_mlp(o, h, w_out, g, w_up, w_down, g_final, final_norm):
    n, d = h.shape
    tm = min(ROW_TILE, n)
    row = lambda i: (i, 0)
    return pl.pallas_call(
        functools.partial(_out_mlp_kernel, final_norm=final_norm),
        grid=(n // tm,),
        in_specs=[pl.BlockSpec((tm, o.shape[1]), row), pl.BlockSpec((tm, d), row),
                  _const_spec(w_out.shape), _const_spec((1, d)),
                  _const_spec(w_up.shape), _const_spec(w_down.shape), _const_spec((1, d))],
        out_specs=pl.BlockSpec((tm, d), row),
        out_shape=jax.ShapeDtypeStruct((n, d), F32),
        compiler_params=pltpu.CompilerParams(
            dimension_semantics=("parallel",), vmem_limit_bytes=VMEM_LIMIT_BYTES),
        name="out_mlp",
    )(o, h, w_out, g, w_up, w_down, g_final)


def _unit_lower_inverse_all(a_lows, row, col, between=()):
    c = a_lows[0].shape[0]
    eye = jnp.where(row == col, 1.0, 0.0).astype(F32)
    negs = [(-a).astype(BF16) for a in a_lows]
    between = list(between)
    ms = None
    s = 1
    while s < c:
        same_pair = (row // (2 * s)) == (col // (2 * s))
        off = same_pair & ((row % (2 * s)) >= s) & ((col % (2 * s)) < s)
        if ms is None:
            ms = [(eye - jnp.where(off, a, 0.0)).astype(BF16) for a in a_lows]
        else:
            ys = [_dot(m, n).astype(BF16) for m, n in zip(ms, negs)]
            ms = [jnp.where(off, _dot(y, m).astype(BF16), m) for y, m in zip(ys, ms)]
            if between:
                between.pop(0)()
        s *= 2
    for work in between:
        work()
    return ms


def _gdn_kernel(q_ref, k_ref, v_ref, z_ref, ba_ref, alog_ref, dtb_ref, ong_ref,
                o_ref, state_ref, *, n_heads, chunk):
    t_idx = pl.program_id(1)
    tt = q_ref.shape[0]
    dh = q_ref.shape[1] // n_heads
    n_chunks = tt // chunk

    @pl.when(t_idx == 0)
    def _():
        state_ref[...] = jnp.zeros_like(state_ref)

    ba = ba_ref[...]
    lane = lax.broadcasted_iota(jnp.int32, ba.shape, 1)
    trow = lax.broadcasted_iota(jnp.int32, ba.shape, 0)
    beta_all = jax.nn.sigmoid(ba)
    sp_in = ba + dtb_ref[...]
    softplus = jnp.maximum(sp_in, 0.0) + jnp.log(1.0 + jnp.exp(-jnp.abs(sp_in)))
    g_all = jnp.where((lane >= n_heads) & (lane < 2 * n_heads), -jnp.exp(alog_ref[...]) * softplus, 0.0)
    gc_all = g_all
    s = 1
    while s < chunk:
        shifted = pltpu.roll(gc_all, s, axis=0)
        gc_all = gc_all + jnp.where((trow % chunk) >= s, shifted, 0.0)
        s *= 2
    gc_all_t = gc_all.T
    beta_all_t = beta_all.T
    g_last_all = jnp.concatenate(
        [jnp.broadcast_to(gc_all[(c + 1) * chunk - 1:(c + 1) * chunk, :], (chunk, LANES)) for c in range(n_chunks)],
        axis=0)
    e_gc_all = jnp.exp(gc_all)
    e_rem_all = jnp.exp(g_last_all - gc_all)
    e_last_all = jnp.exp(g_last_all)

    row = lax.broadcasted_iota(jnp.int32, (chunk, chunk), 0)
    col = lax.broadcasted_iota(jnp.int32, (chunk, chunk), 1)

    rows_of = lambda c: slice(c * chunk, (c + 1) * chunk)
    lanes_of = lambda h: slice(h * dh, (h + 1) * dh)
    heads = range(n_heads)

    def recurrence_steps(c, k16, q16, attn16, sol):
        rows = rows_of(c)
        st = {}

        def read_state():
            st["states"] = [state_ref[h] for h in heads]
            st["s16"] = [s.astype(BF16) for s in st["states"]]
            st["ws"] = [_dot(sol[h][:, dh:].astype(BF16), st["s16"][h]) for h in heads]

        def new_values():
            st["v_new"] = [sol[h][:, :dh] - st["ws"][h] for h in heads]
            vn16 = [v.astype(BF16) for v in st["v_new"]]
            st["o_inter"] = [_dot(q16[h], st["s16"][h]) for h in heads]
            st["o_intra"] = [_dot(attn16[h], vn16[h]) for h in heads]
            st["s_upd"] = [_dot_tn(k16[h], (st["v_new"][h] * e_rem_all[rows, n_heads + h:n_heads + h + 1]
                                           ).astype(BF16)) for h in heads]

        def write_out():
            for h in heads:
                gl = n_heads + h
                state_ref[h] = st["states"][h] * e_last_all[c * chunk:c * chunk + 1, gl:gl + 1] + st["s_upd"][h]
                z = z_ref[rows, lanes_of(h)]
                o = st["o_inter"][h] * e_gc_all[rows, gl:gl + 1] + st["o_intra"][h]
                o = _rms_scale(o) * ong_ref[...] * (z * jax.nn.sigmoid(z))
                o_ref[rows, lanes_of(h)] = o.astype(o_ref.dtype)

        return [read_state, new_values, write_out]

    group_chunks = [list(range(c0, min(c0 + GDN_GROUP_CHUNKS, n_chunks)))
                    for c0 in range(0, n_chunks, GDN_GROUP_CHUNKS)]
    group_pairs = [[(c, h) for c in cs for h in heads] for cs in group_chunks]
    k16 = [[k_ref[rows_of(c), lanes_of(h)] for c, h in pairs] for pairs in group_pairs]
    q16 = [[q_ref[rows_of(c), lanes_of(h)] for c, h in pairs] for pairs in group_pairs]
    kk = [[_dot_nt(k, k) for k in ks] for ks in k16]
    qk = [[_dot_nt(q, k) for q, k in zip(qs, ks)] for qs, ks in zip(q16, k16)]
    pending = []
    for g, pairs in enumerate(group_pairs):
        a_low, attn16 = [], []
        for (c, h), kk_p, qk_p in zip(pairs, kk[g], qk[g]):
            gl = n_heads + h
            gc = gc_all[rows_of(c), gl:gl + 1]
            gc_row = gc_all_t[gl:gl + 1, rows_of(c)]
            decay = jnp.exp(jnp.where(row >= col, gc - gc_row, -jnp.inf))
            a_low.append(jnp.where(row > col, kk_p * (decay * beta_all[rows_of(c), h:h + 1]), 0.0))
            attn16.append((qk_p * decay).astype(BF16))
        t_inv = _unit_lower_inverse_all(a_low, row, col, between=pending)
        sol = []
        for (c, h), t, k in zip(pairs, t_inv, k16[g]):
            gl = n_heads + h
            t_beta = (t.astype(F32) * beta_all_t[h:h + 1, rows_of(c)]).astype(BF16)
            k_dec = (k.astype(F32) * e_gc_all[rows_of(c), gl:gl + 1]).astype(BF16)
            sol.append(_dot(t_beta, jnp.concatenate([v_ref[rows_of(c), lanes_of(h)], k_dec], axis=-1)))
        pending = []
        for ci, c in enumerate(group_chunks[g]):
            part = slice(ci * n_heads, (ci + 1) * n_heads)
            pending += recurrence_steps(c, k16[g][part], q16[g][part], attn16[part], sol[part])
    for work in pending:
        work()


def _gdn_core(q, k, v, z, ba, a_log_row, dt_bias_row, out_norm_g, batch, seq):
    n_heads, dh = DN_HEADS, DN_HEAD_DIM
    width = n_heads * dh
    tt = min(GDN_TIME_TILE, seq)
    nt = seq // tt
    tile = lambda b, t: (b * nt + t, 0)
    return pl.pallas_call(
        functools.partial(_gdn_kernel, n_heads=n_heads, chunk=min(GDN_CHUNK, tt)),
        grid=(batch, nt),
        in_specs=[pl.BlockSpec((tt, width), tile)] * 4 + [pl.BlockSpec((tt, LANES), tile),
                  _const_spec((1, LANES)), _const_spec((1, LANES)), _const_spec((1, dh))],
        out_specs=pl.BlockSpec((tt, width), tile),
        out_shape=jax.ShapeDtypeStruct((batch * seq, width), BF16),
        scratch_shapes=[pltpu.VMEM((n_heads, dh, dh), F32)],
        compiler_params=pltpu.CompilerParams(
            dimension_semantics=("parallel", "arbitrary"), vmem_limit_bytes=VMEM_LIMIT_BYTES),
        name="gdn_core",
    )(q, k, v, z, ba, a_log_row, dt_bias_row, out_norm_g)


def _split3_f32(x):
    hi = x.astype(BF16).astype(F32)
    mid = (x - hi).astype(BF16).astype(F32)
    return hi, mid, ((x - hi) - mid).astype(BF16).astype(F32)


def _moba_kernel(q_ref, k_ref, v_ref, o_ref, kmean_ref, vt_ref, maskb_ref, kaug_ref, qaug_ref,
                 m_ref, acc_ref, *, n_heads, top_k):
    i = pl.program_id(1)
    blk = q_ref.shape[0]
    seq = k_ref.shape[0]
    dh = k_ref.shape[1] // n_heads
    nb = seq // blk
    scale2 = dh ** -0.5 * math.log2(math.e)
    heads = range(n_heads)
    lanes = [slice(h * dh, (h + 1) * dh) for h in heads]
    slope_raw = [2.0 ** (-8.0 * (h + 1) / n_heads) * dh ** 0.5 for h in heads]

    krow = lax.broadcasted_iota(jnp.int32, (blk, blk), 0)
    qcol = lax.broadcasted_iota(jnp.int32, (blk, blk), 1)
    aug_lane = lax.broadcasted_iota(jnp.int32, (blk, dh), 1)

    @pl.when(i == 0)
    def _():
        r = lax.broadcasted_iota(jnp.int32, (nb, seq), 0)
        c = lax.broadcasted_iota(jnp.int32, (nb, seq), 1)
        ind = jnp.where(c // blk == r, 1.0, 0.0).astype(BF16)
        key_pos = lax.broadcasted_iota(jnp.int32, (blk, dh), 0).astype(F32)
        for h in heads:
            kmean_ref[h] = (_dot(ind, k_ref[:, lanes[h]]) * (1.0 / blk)).astype(BF16)
            hi, mid, lo = _split3_f32(slope_raw[h] * key_pos)
            kaug_ref[h] = jnp.where(aug_lane == 0, hi, jnp.where(aug_lane == 1, mid,
                                                                 jnp.where(aug_lane == 2, lo, 0.0))).astype(BF16)
            for j in range(nb):
                vt = v_ref[j * blk:(j + 1) * blk, lanes[h]].astype(F32).T.astype(BF16)
                vt_ref[h, j] = jnp.concatenate([vt, jnp.ones((MOBA_SUM_ROWS, blk), BF16)], axis=0)

    qs = [q_ref[:, lanes[h]] for h in heads]
    ones_cols = jnp.where(aug_lane < 3, 1.0, 0.0).astype(BF16)
    for h in heads:
        qaug_ref[h] = jnp.concatenate([qs[h], ones_cols], axis=1)

    gates = [_dot_nt(kmean_ref[h], qs[h]) for h in heads]
    bidx = lax.broadcasted_iota(jnp.int32, (nb, blk), 0)
    past = bidx < i
    for h in heads:
        gate = jnp.where(past, gates[h], -jnp.inf)
        rank = jnp.zeros(gate.shape, F32)
        for jp in range(nb):
            other = gate[jp:jp + 1, :]
            beats = (other > gate) | ((other == gate) & (jp < bidx))
            rank = rank + jnp.where(beats, 1.0, 0.0)
        maskb_ref[h] = jnp.where(past & (rank < top_k), 0.0, -jnp.inf)

    def pipelined(tiles, score_mm, rest):
        raw = {}
        for step in range(len(tiles) + MOBA_SCORE_LEAD):
            if step < len(tiles):
                raw[step] = score_mm(*tiles[step])
            if step >= MOBA_SCORE_LEAD:
                rest(*tiles[step - MOBA_SCORE_LEAD], raw.pop(step - MOBA_SCORE_LEAD))

    def scores(h, j):
        rows = pl.ds(pl.multiple_of(j * blk, blk), blk)
        k_aug = jnp.concatenate([k_ref[rows, lanes[h]], kaug_ref[h]], axis=1)
        return _dot_nt(k_aug, qaug_ref[h])

    def own_rest(h, j, raw):
        s = jnp.where(krow <= qcol, raw, -jnp.inf)
        m = jnp.max(s, axis=0, keepdims=True)
        m_ref[h] = m
        acc_ref[h] = _dot(vt_ref[h, j], jnp.exp2((s - m) * scale2).astype(BF16))

    pipelined([(h, i) for h in heads], scores, own_rest)

    def past_blocks(js):
        def rest(h, j, raw):
            row_bias = maskb_ref[h, pl.ds(j, 1), :] + slope_raw[h] * ((j - i) * blk).astype(F32)
            m_old = m_ref[h]
            m_new = jnp.maximum(m_old, jnp.max(raw, axis=0, keepdims=True) + row_bias)
            m_ref[h] = m_new
            p = jnp.exp2((raw + (row_bias - m_new)) * scale2).astype(BF16)
            acc_ref[h] = acc_ref[h] * jnp.exp2((m_old - m_new) * scale2) + _dot(vt_ref[h, j], p)

        pipelined([(h, j) for j in js for h in heads], scores, rest)

    def pair_body(jj, carry):
        past_blocks([2 * jj, 2 * jj + 1])
        return carry

    lax.fori_loop(0, i >> 1, pair_body, 0)

    @pl.when((i & 1) == 1)
    def _():
        past_blocks([i - 1])

    for h in heads:
        acc = acc_ref[h]
        o_ref[:, lanes[h]] = (acc[:dh] / acc[dh:dh + 1]).T.astype(o_ref.dtype)


def _moba_attn(qkv, batch, seq):
    n_heads = MB_HEADS
    width = qkv.shape[1] // 3
    dh = width // n_heads
    blk = min(MB_BLOCK, seq)
    nb = seq // blk
    return pl.pallas_call(
        functools.partial(_moba_kernel, n_heads=n_heads, top_k=min(MB_TOPK, nb)),
        grid=(batch, nb),
        in_specs=[pl.BlockSpec((blk, width), lambda b, i: (b * nb + i, 0)),
                  pl.BlockSpec((seq, width), lambda b, i: (b, 1)),
                  pl.BlockSpec((seq, width), lambda b, i: (b, 2))],
        out_specs=pl.BlockSpec((blk, width), lambda b, i: (b * nb + i, 0)),
        out_shape=jax.ShapeDtypeStruct((batch * seq, width), BF16),
        scratch_shapes=[pltpu.VMEM((n_heads, nb, dh), BF16),
                        pltpu.VMEM((n_heads, nb, dh + MOBA_SUM_ROWS, blk), BF16),
                        pltpu.VMEM((n_heads, nb, blk), F32),
                        pltpu.VMEM((n_heads, blk, dh), BF16),
                        pltpu.VMEM((n_heads, blk, 2 * dh), BF16),
                        pltpu.VMEM((n_heads, 1, blk), F32),
                        pltpu.VMEM((n_heads, dh + MOBA_SUM_ROWS, blk), F32)],
        compiler_params=pltpu.CompilerParams(
            dimension_semantics=("parallel", "arbitrary"), vmem_limit_bytes=VMEM_LIMIT_BYTES),
        name="moba_attn",
    )(qkv, qkv, qkv)


def _pad_lanes(row_vec, offset):
    return jnp.zeros((1, LANES), F32).at[0, offset:offset + row_vec.shape[0]].set(row_vec.astype(F32))


def kernel(x, mix_norm_g, mlp_norm_g, dn_w_in, dn_conv_w, dn_a_log, dn_dt_bias, dn_out_norm_g, dn_w_out,
           mb_w_in, mb_w_out, mlp_w_up, mlp_w_down, final_norm_g):
    batch, seq, d = x.shape
    n = batch * seq
    h = x.reshape(n, d)
    dn_dim = DN_HEADS * DN_HEAD_DIM
    row = lambda v: v.reshape(1, -1).astype(F32)

    w_main = dn_w_in[0, :, :4 * dn_dim].astype(BF16)
    w_ba = jnp.zeros((d, LANES), F32).at[:, :2 * DN_HEADS].set(dn_w_in[0, :, 4 * dn_dim:]).astype(BF16)
    q, k, v, z, ba = _gdn_proj(h, row(mix_norm_g[0]), w_main, w_ba, dn_conv_w[0].astype(F32), seq)
    o = _gdn_core(q, k, v, z, ba, _pad_lanes(dn_a_log[0], DN_HEADS), _pad_lanes(dn_dt_bias[0], DN_HEADS),
                  row(dn_out_norm_g[0]), batch, seq)
    h = _out_mlp(o, h, dn_w_out[0].astype(BF16), row(mlp_norm_g[0]), mlp_w_up[0].astype(BF16),
                 mlp_w_down[0].astype(BF16), row(final_norm_g), final_norm=False)

    (qkv,) = _rms_proj(h, row(mix_norm_g[1]), [mb_w_in[0].astype(BF16)], [BF16])
    o = _moba_attn(qkv, batch, seq)
    h = _out_mlp(o, h, mb_w_out[0].astype(BF16), row(mlp_norm_g[1]), mlp_w_up[1].astype(BF16),
                 mlp_w_down[1].astype(BF16), row(final_norm_g), final_norm=True)
    return h.reshape(batch, seq, d)
```

```python
import functools
import math

import jax
import jax.numpy as jnp
from jax import lax
from jax.experimental import pallas as pl
from jax.experimental.pallas import tpu as pltpu

F32 = jnp.float32
BF16 = jnp.bfloat16

RMS_EPS = 1e-6
L2_EPS = 1e-6

DN_HEADS = 8
DN_HEAD_DIM = 128
DN_CONV = 4
GDN_CHUNK = 128
MB_HEADS = 8
MB_BLOCK = 256
MB_TOPK = 3

LANES = 128
SUBLANES = 8
VMEM_LIMIT_BYTES = 56 * 1024 * 1024

ROW_TILE = 512
COL_TILE = 512
GDN_PROJ_COL_TILE = 256
GDN_TIME_TILE = 512
GDN_GROUP_CHUNKS = 2
MOBA_SUM_ROWS = 16
MOBA_SCORE_LEAD = 4


def _dot(a, b):
    return jnp.dot(a, b, preferred_element_type=F32)


def _dot_nt(a, b):
    return lax.dot_general(a, b, (((1,), (1,)), ((), ())), preferred_element_type=F32)


def _dot_tn(a, b):
    return lax.dot_general(a, b, (((0,), (0,)), ((), ())), preferred_element_type=F32)


def _rms_scale(x):
    return x * lax.rsqrt(jnp.mean(x * x, axis=-1, keepdims=True) + RMS_EPS)


def _const_spec(shape):
    return pl.BlockSpec(shape, lambda *_: (0,) * len(shape), pipeline_mode=pl.Buffered(1))


def _rms_proj_kernel(x_ref, g_ref, *refs, n_w):
    w_refs, o_refs = refs[:n_w], refs[n_w:]
    hn = (_rms_scale(x_ref[...]) * g_ref[...]).astype(BF16)
    for w_ref, o_ref in zip(w_refs, o_refs):
        n_cols = w_ref.shape[1]
        tn = min(COL_TILE, n_cols)
        for c in range(n_cols // tn):
            cols = slice(c * tn, (c + 1) * tn)
            o_ref[:, cols] = _dot(hn, w_ref[:, cols]).astype(o_ref.dtype)


def _rms_proj(x, g, ws, out_dtypes):
    n, d = x.shape
    tm = min(ROW_TILE, n)
    in_specs = [pl.BlockSpec((tm, d), lambda i: (i, 0)), _const_spec((1, d))]
    in_specs += [_const_spec(w.shape) for w in ws]
    out_specs = [pl.BlockSpec((tm, w.shape[1]), lambda i: (i, 0)) for w in ws]
    out_shape = [jax.ShapeDtypeStruct((n, w.shape[1]), dt) for w, dt in zip(ws, out_dtypes)]
    return pl.pallas_call(
        functools.partial(_rms_proj_kernel, n_w=len(ws)),
        grid=(n // tm,),
        in_specs=in_specs,
        out_specs=out_specs,
        out_shape=out_shape,
        compiler_params=pltpu.CompilerParams(
            dimension_semantics=("parallel",), vmem_limit_bytes=VMEM_LIMIT_BYTES),
        name="rms_proj",
    )(x, g, *ws)


def _gdn_proj_kernel(x_ref, g_ref, w_ref, wba_ref, cw_ref, q_ref, k_ref, v_ref, z_ref, ba_ref, halo_ref,
                     *, n_heads, tiles_per_seq):
    tm = x_ref.shape[0]
    width = q_ref.shape[1]
    dh = width // n_heads
    tc = min(GDN_PROJ_COL_TILE, width)
    groups = tm // SUBLANES
    hn = (_rms_scale(x_ref[...]) * g_ref[...]).astype(BF16)

    @pl.when(lax.rem(pl.program_id(0), tiles_per_seq) == 0)
    def _():
        halo_ref[...] = jnp.zeros_like(halo_ref)

    sub = lax.broadcasted_iota(jnp.int32, (groups, SUBLANES, tc), 1)
    for sec, o_ref in enumerate((q_ref, k_ref, v_ref)):
        for c in range(width // tc):
            cols = slice(sec * width + c * tc, sec * width + (c + 1) * tc)
            y = _dot(hn, w_ref[:, cols])
            yg = jnp.concatenate([y, halo_ref[:, cols]], axis=0).reshape(groups + 1, SUBLANES, tc)
            halo_ref[:, cols] = y[tm - SUBLANES:, :]
            acc = y * cw_ref[DN_CONV - 1:DN_CONV, cols]
            for s in range(1, DN_CONV):
                r = pltpu.roll(yg, s, axis=1)
                prev = jnp.concatenate([r[groups:], r[:groups - 1]], axis=0)
                shifted = jnp.where(sub >= s, r[:groups], prev).reshape(tm, tc)
                acc = acc + shifted * cw_ref[DN_CONV - 1 - s:DN_CONV - s, cols]
            a = acc * jax.nn.sigmoid(acc)
            if sec < 2:
                parts = []
                for hh in range(tc // dh):
                    p = a[:, hh * dh:(hh + 1) * dh]
                    inv = lax.rsqrt(jnp.sum(p * p, axis=-1, keepdims=True) + L2_EPS)
                    parts.append(p * (inv * dh ** -0.5 if sec == 0 else inv))
                a = jnp.concatenate(parts, axis=1)
            o_ref[:, c * tc:(c + 1) * tc] = a.astype(o_ref.dtype)
    for c in range(width // tc):
        z_ref[:, c * tc:(c + 1) * tc] = _dot(hn, w_ref[:, 3 * width + c * tc:3 * width + (c + 1) * tc])
    ba_ref[...] = _dot(hn, wba_ref[...])


def _gdn_proj(x, g, w_main, w_ba, conv_w, seq):
    n, d = x.shape
    width = w_main.shape[1] // 4
    tm = min(ROW_TILE, seq)
    row = lambda i: (i, 0)
    return pl.pallas_call(
        functools.partial(_gdn_proj_kernel, n_heads=DN_HEADS, tiles_per_seq=seq // tm),
        grid=(n // tm,),
        in_specs=[pl.BlockSpec((tm, d), row), _const_spec((1, d)), _const_spec(w_main.shape),
                  _const_spec(w_ba.shape), _const_spec(conv_w.shape)],
        out_specs=[pl.BlockSpec((tm, width), row)] * 4 + [pl.BlockSpec((tm, LANES), row)],
        out_shape=[jax.ShapeDtypeStruct((n, width), BF16)] * 3
                  + [jax.ShapeDtypeStruct((n, width), F32), jax.ShapeDtypeStruct((n, LANES), F32)],
        scratch_shapes=[pltpu.VMEM((SUBLANES, 3 * width), F32)],
        compiler_params=pltpu.CompilerParams(
            dimension_semantics=("arbitrary",), vmem_limit_bytes=VMEM_LIMIT_BYTES),
        name="gdn_proj",
    )(x, g, w_main, w_ba, conv_w)


def _out_mlp_kernel(o_ref, h_ref, wout_ref, g_ref, wup_ref, wdn_ref, gf_ref, out_ref, *, final_norm):
    h1 = h_ref[...] + _dot(o_ref[...], wout_ref[...])
    hn = (_rms_scale(h1) * g_ref[...]).astype(BF16)
    out_ref[...] = h1
    d_ff = wup_ref.shape[1]
    tf = min(COL_TILE, d_ff)
    for c in range(d_ff // tf):
        cols = slice(c * tf, (c + 1) * tf)
        act = jnp.square(jnp.maximum(_dot(hn, wup_ref[:, cols]), 0.0)).astype(BF16)
        out_ref[...] += _dot(act, wdn_ref[cols, :])
    if final_norm:
        out_ref[...] = _rms_scale(out_ref[...]) * gf_ref[...]


def _out_mlp(o, h, w_out, g, w_up, w_down, g_final, final_norm):
    n, d = h.shape
    tm = min(ROW_TILE, n)
    row = lambda i: (i, 0)
    return pl.pallas_call(
        functools.partial(_out_mlp_kernel, final_norm=final_norm),
        grid=(n // tm,),
        in_specs=[pl.BlockSpec((tm, o.shape[1]), row), pl.BlockSpec((tm, d), row),
                  _const_spec(w_out.shape), _const_spec((1, d)),
                  _const_spec(w_up.shape), _const_spec(w_down.shape), _const_spec((1, d))],
        out_specs=pl.BlockSpec((tm, d), row),
        out_shape=jax.ShapeDtypeStruct((n, d), F32),
        compiler_params=pltpu.CompilerParams(
            dimension_semantics=("parallel",), vmem_limit_bytes=VMEM_LIMIT_BYTES),
        name="out_mlp",
    )(o, h, w_out, g, w_up, w_down, g_final)


def _unit_lower_inverse_all(a_lows, row, col, between=()):
    c = a_lows[0].shape[0]
    eye = jnp.where(row == col, 1.0, 0.0).astype(F32)
    negs = [(-a).astype(BF16) for a in a_lows]
    between = list(between)
    ms = None
    s = 1
    while s < c:
        same_pair = (row // (2 * s)) == (col // (2 * s))
        off = same_pair & ((row % (2 * s)) >= s) & ((col % (2 * s)) < s)
        if ms is None:
            ms = [(eye - jnp.where(off, a, 0.0)).astype(BF16) for a in a_lows]
        else:
            ys = [_dot(m, n).astype(BF16) for m, n in zip(ms, negs)]
            ms = [jnp.where(off, _dot(y, m).astype(BF16), m) for y, m in zip(ys, ms)]
            if between:
                between.pop(0)()
        s *= 2
    for work in between:
        work()
    return ms


def _gdn_kernel(q_ref, k_ref, v_ref, z_ref, ba_ref, alog_ref, dtb_ref, ong_ref,
                o_ref, state_ref, *, n_heads, chunk):
    t_idx = pl.program_id(1)
    tt = q_ref.shape[0]
    dh = q_ref.shape[1] // n_heads
    n_chunks = tt // chunk

    @pl.when(t_idx == 0)
    def _():
        state_ref[...] = jnp.zeros_like(state_ref)

    ba = ba_ref[...]
    lane = lax.broadcasted_iota(jnp.int32, ba.shape, 1)
    trow = lax.broadcasted_iota(jnp.int32, ba.shape, 0)
    beta_all = jax.nn.sigmoid(ba)
    sp_in = ba + dtb_ref[...]
    softplus = jnp.maximum(sp_in, 0.0) + jnp.log(1.0 + jnp.exp(-jnp.abs(sp_in)))
    g_all = jnp.where((lane >= n_heads) & (lane < 2 * n_heads), -jnp.exp(alog_ref[...]) * softplus, 0.0)
    gc_all = g_all
    s = 1
    while s < chunk:
        shifted = pltpu.roll(gc_all, s, axis=0)
        gc_all = gc_all + jnp.where((trow % chunk) >= s, shifted, 0.0)
        s *= 2
    gc_all_t = gc_all.T
    beta_all_t = beta_all.T
    g_last_all = jnp.concatenate(
        [jnp.broadcast_to(gc_all[(c + 1) * chunk - 1:(c + 1) * chunk, :], (chunk, LANES)) for c in range(n_chunks)],
        axis=0)
    e_gc_all = jnp.exp(gc_all)
    e_rem_all = jnp.exp(g_last_all - gc_all)
    e_last_all = jnp.exp(g_last_all)

    row = lax.broadcasted_iota(jnp.int32, (chunk, chunk), 0)
    col = lax.broadcasted_iota(jnp.int32, (chunk, chunk), 1)

    rows_of = lambda c: slice(c * chunk, (c + 1) * chunk)
    lanes_of = lambda h: slice(h * dh, (h + 1) * dh)
    heads = range(n_heads)

    def recurrence_steps(c, k16, q16, attn16, sol):
        rows = rows_of(c)
        st = {}

        def read_state():
            st["states"] = [state_ref[h] for h in heads]
            st["s16"] = [s.astype(BF16) for s in st["states"]]
            st["ws"] = [_dot(sol[h][:, dh:].astype(BF16), st["s16"][h]) for h in heads]

        def new_values():
            st["v_new"] = [sol[h][:, :dh] - st["ws"][h] for h in heads]
            vn16 = [v.astype(BF16) for v in st["v_new"]]
            st["o_inter"] = [_dot(q16[h], st["s16"][h]) for h in heads]
            st["o_intra"] = [_dot(attn16[h], vn16[h]) for h in heads]
            st["s_upd"] = [_dot_tn(k16[h], (st["v_new"][h] * e_rem_all[rows, n_heads + h:n_heads + h + 1]
                                           ).astype(BF16)) for h in heads]

        def write_out():
            for h in heads:
                gl = n_heads + h
                state_ref[h] = st["states"][h] * e_last_all[c * chunk:c * chunk + 1, gl:gl + 1] + st["s_upd"][h]
                z = z_ref[rows, lanes_of(h)]
                o = st["o_inter"][h] * e_gc_all[rows, gl:gl + 1] + st["o_intra"][h]
                o = _rms_scale(o) * ong_ref[...] * (z * jax.nn.sigmoid(z))
                o_ref[rows, lanes_of(h)] = o.astype(o_ref.dtype)

        return [read_state, new_values, write_out]

    group_chunks = [list(range(c0, min(c0 + GDN_GROUP_CHUNKS, n_chunks)))
                    for c0 in range(0, n_chunks, GDN_GROUP_CHUNKS)]
    group_pairs = [[(c, h) for c in cs for h in heads] for cs in group_chunks]
    k16 = [[k_ref[rows_of(c), lanes_of(h)] for c, h in pairs] for pairs in group_pairs]
    q16 = [[q_ref[rows_of(c), lanes_of(h)] for c, h in pairs] for pairs in group_pairs]
    kk = [[_dot_nt(k, k) for k in ks] for ks in k16]
    qk = [[_dot_nt(q, k) for q, k in zip(qs, ks)] for qs, ks in zip(q16, k16)]
    pending = []
    for g, pairs in enumerate(group_pairs):
        a_low, attn16 = [], []
        for (c, h), kk_p, qk_p in zip(pairs, kk[g], qk[g]):
            gl = n_heads + h
            gc = gc_all[rows_of(c), gl:gl + 1]
            gc_row = gc_all_t[gl:gl + 1, rows_of(c)]
            decay = jnp.exp(jnp.where(row >= col, gc - gc_row, -jnp.inf))
            a_low.append(jnp.where(row > col, kk_p * (decay * beta_all[rows_of(c), h:h + 1]), 0.0))
            attn16.append((qk_p * decay).astype(BF16))
        t_inv = _unit_lower_inverse_all(a_low, row, col, between=pending)
        sol = []
        for (c, h), t, k in zip(pairs, t_inv, k16[g]):
            gl = n_heads + h
            t_beta = (t.astype(F32) * beta_all_t[h:h + 1, rows_of(c)]).astype(BF16)
            k_dec = (k.astype(F32) * e_gc_all[rows_of(c), gl:gl + 1]).astype(BF16)
            sol.append(_dot(t_beta, jnp.concatenate([v_ref[rows_of(c), lanes_of(h)], k_dec], axis=-1)))
        pending = []
        for ci, c in enumerate(group_chunks[g]):
            part = slice(ci * n_heads, (ci + 1) * n_heads)
            pending += recurrence_steps(c, k16[g][part], q16[g][part], attn16[part], sol[part])
    for work in pending:
        work()


def _gdn_core(q, k, v, z, ba, a_log_row, dt_bias_row, out_norm_g, batch, seq):
    n_heads, dh = DN_HEADS, DN_HEAD_DIM
    width = n_heads * dh
    tt = min(GDN_TIME_TILE, seq)
    nt = seq // tt
    tile = lambda b, t: (b * nt + t, 0)
    return pl.pallas_call(
        functools.partial(_gdn_kernel, n_heads=n_heads, chunk=min(GDN_CHUNK, tt)),
        grid=(batch, nt),
        in_specs=[pl.BlockSpec((tt, width), tile)] * 4 + [pl.BlockSpec((tt, LANES), tile),
                  _const_spec((1, LANES)), _const_spec((1, LANES)), _const_spec((1, dh))],
        out_specs=pl.BlockSpec((tt, width), tile),
        out_shape=jax.ShapeDtypeStruct((batch * seq, width), BF16),
        scratch_shapes=[pltpu.VMEM((n_heads, dh, dh), F32)],
        compiler_params=pltpu.CompilerParams(
            dimension_semantics=("parallel", "arbitrary"), vmem_limit_bytes=VMEM_LIMIT_BYTES),
        name="gdn_core",
    )(q, k, v, z, ba, a_log_row, dt_bias_row, out_norm_g)


def _split3_f32(x):
    hi = x.astype(BF16).astype(F32)
    mid = (x - hi).astype(BF16).astype(F32)
    return hi, mid, ((x - hi) - mid).astype(BF16).astype(F32)


def _moba_kernel(q_ref, k_ref, v_ref, o_ref, kmean_ref, vt_ref, maskb_ref, kaug_ref, qaug_ref,
                 m_ref, acc_ref, *, n_heads, top_k):
    i = pl.program_id(1)
    blk = q_ref.shape[0]
    seq = k_ref.shape[0]
    dh = k_ref.shape[1] // n_heads
    nb = seq // blk
    scale2 = dh ** -0.5 * math.log2(math.e)
    heads = range(n_heads)
    lanes = [slice(h * dh, (h + 1) * dh) for h in heads]
    slope2 = [2.0 ** (-8.0 * (h + 1) / n_heads) * math.log2(math.e) for h in heads]

    krow = lax.broadcasted_iota(jnp.int32, (blk, blk), 0)
    qcol = lax.broadcasted_iota(jnp.int32, (blk, blk), 1)
    aug_lane = lax.broadcasted_iota(jnp.int32, (blk, dh), 1)

    @pl.when(i == 0)
    def _():
        r = lax.broadcasted_iota(jnp.int32, (nb, seq), 0)
        c = lax.broadcasted_iota(jnp.int32, (nb, seq), 1)
        ind = jnp.where(c // blk == r, 1.0, 0.0).astype(BF16)
        key_pos = lax.broadcasted_iota(jnp.int32, (blk, dh), 0).astype(F32)
        for h in heads:
            kmean_ref[h] = (_dot(ind, k_ref[:, lanes[h]]) * (1.0 / blk)).astype(BF16)
            hi, mid, lo = _split3_f32(slope2[h] * key_pos)
            kaug_ref[h] = jnp.where(aug_lane == 0, hi, jnp.where(aug_lane == 1, mid,
                                                                 jnp.where(aug_lane == 2, lo, 0.0))).astype(BF16)
            for j in range(nb):
                vt = v_ref[j * blk:(j + 1) * blk, lanes[h]].astype(F32).T.astype(BF16)
                vt_ref[h, j] = jnp.concatenate([vt, jnp.ones((MOBA_SUM_ROWS, blk), BF16)], axis=0)

    qs = [q_ref[:, lanes[h]] for h in heads]
    ones_cols = jnp.where(aug_lane < 3, 1.0, 0.0).astype(BF16)
    for h in heads:
        qaug_ref[h] = jnp.concatenate([(qs[h].astype(F32) * scale2).astype(BF16), ones_cols], axis=1)

    gates = [_dot_nt(kmean_ref[h], qs[h]) for h in heads]
    bidx = lax.broadcasted_iota(jnp.int32, (nb, blk), 0)
    past = bidx < i
    for h in heads:
        gate = jnp.where(past, gates[h], -jnp.inf)
        rank = jnp.zeros(gate.shape, F32)
        for jp in range(nb):
            other = gate[jp:jp + 1, :]
            beats = (other > gate) | ((other == gate) & (jp < bidx))
            rank = rank + jnp.where(beats, 1.0, 0.0)
        maskb_ref[h] = jnp.where(past & (rank < top_k), 0.0, -jnp.inf)

    def pipelined(tiles, score_mm, rest):
        raw = {}
        for step in range(len(tiles) + MOBA_SCORE_LEAD):
            if step < len(tiles):
                raw[step] = score_mm(*tiles[step])
            if step >= MOBA_SCORE_LEAD:
                rest(*tiles[step - MOBA_SCORE_LEAD], raw.pop(step - MOBA_SCORE_LEAD))

    def scores(h, j):
        rows = pl.ds(pl.multiple_of(j * blk, blk), blk)
        k_aug = jnp.concatenate([k_ref[rows, lanes[h]], kaug_ref[h]], axis=1)
        return _dot_nt(k_aug, qaug_ref[h])

    def own_rest(h, j, raw):
        s = jnp.where(krow <= qcol, raw, -jnp.inf)
        m = jnp.max(s, axis=0, keepdims=True)
        m_ref[h] = m
        acc_ref[h] = _dot(vt_ref[h, j], jnp.exp2(s - m).astype(BF16))

    pipelined([(h, i) for h in heads], scores, own_rest)

    def past_blocks(js):
        def rest(h, j, raw):
            row_bias = maskb_ref[h, pl.ds(j, 1), :] + slope2[h] * ((j - i) * blk).astype(F32)
            m_old = m_ref[h]
            m_new = jnp.maximum(m_old, jnp.max(raw, axis=0, keepdims=True) + row_bias)
            m_ref[h] = m_new
            p = jnp.exp2(raw + (row_bias - m_new)).astype(BF16)
            acc_ref[h] = acc_ref[h] * jnp.exp2(m_old - m_new) + _dot(vt_ref[h, j], p)

        pipelined([(h, j) for j in js for h in heads], scores, rest)

    def quad_body(jj, carry):
        past_blocks([4 * jj + d for d in range(4)])
        return carry

    lax.fori_loop(0, i >> 2, quad_body, 0)

    @pl.when((i & 2) == 2)
    def _():
        past_blocks([(i >> 2) * 4, (i >> 2) * 4 + 1])

    @pl.when((i & 1) == 1)
    def _():
        past_blocks([i - 1])

    for h in heads:
        acc = acc_ref[h]
        o_ref[:, lanes[h]] = (acc[:dh] / acc[dh:dh + 1]).T.astype(o_ref.dtype)


def _moba_attn(qkv, batch, seq):
    n_heads = MB_HEADS
    width = qkv.shape[1] // 3
    dh = width // n_heads
    blk = min(MB_BLOCK, seq)
    nb = seq // blk
    return pl.pallas_call(
        functools.partial(_moba_kernel, n_heads=n_heads, top_k=min(MB_TOPK, nb)),
        grid=(batch, nb),
        in_specs=[pl.BlockSpec((blk, width), lambda b, i: (b * nb + i, 0)),
                  pl.BlockSpec((seq, width), lambda b, i: (b, 1)),
                  pl.BlockSpec((seq, width), lambda b, i: (b, 2))],
        out_specs=pl.BlockSpec((blk, width), lambda b, i: (b * nb + i, 0)),
        out_shape=jax.ShapeDtypeStruct((batch * seq, width), BF16),
        scratch_shapes=[pltpu.VMEM((n_heads, nb, dh), BF16),
                        pltpu.VMEM((n_heads, nb, dh + MOBA_SUM_ROWS, blk), BF16),
                        pltpu.VMEM((n_heads, nb, blk), F32),
                        pltpu.VMEM((n_heads, blk, dh), BF16),
                        pltpu.VMEM((n_heads, blk, 2 * dh), BF16),
                        pltpu.VMEM((n_heads, 1, blk), F32),
                        pltpu.VMEM((n_heads, dh + MOBA_SUM_ROWS, blk), F32)],
        compiler_params=pltpu.CompilerParams(
            dimension_semantics=("parallel", "arbitrary"), vmem_limit_bytes=VMEM_LIMIT_BYTES),
        name="moba_attn",
    )(qkv, qkv, qkv)


def _pad_lanes(row_vec, offset):
    return jnp.zeros((1, LANES), F32).at[0, offset:offset + row_vec.shape[0]].set(row_vec.astype(F32))


def kernel(x, mix_norm_g, mlp_norm_g, dn_w_in, dn_conv_w, dn_a_log, dn_dt_bias, dn_out_norm_g, dn_w_out,
           mb_w_in, mb_w_out, mlp_w_up, mlp_w_down, final_norm_g):
    batch, seq, d = x.shape
    n = batch * seq
    h = x.reshape(n, d)
    dn_dim = DN_HEADS * DN_HEAD_DIM
    row = lambda v: v.reshape(1, -1).astype(F32)

    w_main = dn_w_in[0, :, :4 * dn_dim].astype(BF16)
    w_ba = jnp.zeros((d, LANES), F32).at[:, :2 * DN_HEADS].set(dn_w_in[0, :, 4 * dn_dim:]).astype(BF16)
    q, k, v, z, ba = _gdn_proj(h, row(mix_norm_g[0]), w_main, w_ba, dn_conv_w[0].astype(F32), seq)
    o = _gdn_core(q, k, v, z, ba, _pad_lanes(dn_a_log[0], DN_HEADS), _pad_lanes(dn_dt_bias[0], DN_HEADS),
                  row(dn_out_norm_g[0]), batch, seq)
    h = _out_mlp(o, h, dn_w_out[0].astype(BF16), row(mlp_norm_g[0]), mlp_w_up[0].astype(BF16),
                 mlp_w_down[0].astype(BF16), row(final_norm_g), final_norm=False)

    (qkv,) = _rms_proj(h, row(mix_norm_g[1]), [mb_w_in[0].astype(BF16)], [BF16])
    o = _moba_attn(qkv, batch, seq)
    h = _out_mlp(o, h, mb_w_out[0].astype(BF16), row(mlp_norm_g[1]), mlp_w_up[1].astype(BF16),
                 mlp_w_down[1].astype(BF16), row(final_norm_g), final_norm=True)
    return h.reshape(batch, seq, d)
```

```python
import functools
import math

import jax
import jax.numpy as jnp
from jax import lax
from jax.experimental import pallas as pl
from jax.experimental.pallas import tpu as pltpu

F32 = jnp.float32
BF16 = jnp.bfloat16

RMS_EPS = 1e-6
L2_EPS = 1e-6

DN_HEADS = 8
DN_HEAD_DIM = 128
DN_CONV = 4
GDN_CHUNK = 128
MB_HEADS = 8
MB_BLOCK = 256
MB_TOPK = 3

LANES = 128
SUBLANES = 8
VMEM_LIMIT_BYTES = 56 * 1024 * 1024

ROW_TILE = 512
COL_TILE = 512
GDN_PROJ_COL_TILE = 256
GDN_TIME_TILE = 512
GDN_GROUP_CHUNKS = 2
MOBA_SUM_ROWS = 16
MOBA_SCORE_LEAD = 4


def _dot(a, b):
    return jnp.dot(a, b, preferred_element_type=F32)


def _dot_nt(a, b):
    return lax.dot_general(a, b, (((1,), (1,)), ((), ())), preferred_element_type=F32)


def _dot_tn(a, b):
    return lax.dot_general(a, b, (((0,), (0,)), ((), ())), preferred_element_type=F32)


def _rms_scale(x):
    return x * lax.rsqrt(jnp.mean(x * x, axis=-1, keepdims=True) + RMS_EPS)


def _const_spec(shape):
    return pl.BlockSpec(shape, lambda *_: (0,) * len(shape), pipeline_mode=pl.Buffered(1))


def _layer_spec(stacked, layer):
    return pl.BlockSpec((None,) + stacked.shape[1:], lambda *_: (layer, 0, 0), pipeline_mode=pl.Buffered(1))


def _rms_proj_kernel(x_ref, g_ref, *refs, n_w):
    w_refs, o_refs = refs[:n_w], refs[n_w:]
    hn = (_rms_scale(x_ref[...]) * g_ref[...]).astype(BF16)
    for w_ref, o_ref in zip(w_refs, o_refs):
        n_cols = w_ref.shape[1]
        tn = min(COL_TILE, n_cols)
        for c in range(n_cols // tn):
            cols = slice(c * tn, (c + 1) * tn)
            o_ref[:, cols] = _dot(hn, w_ref[:, cols]).astype(o_ref.dtype)


def _rms_proj(x, g, ws, out_dtypes):
    n, d = x.shape
    tm = min(ROW_TILE, n)
    in_specs = [pl.BlockSpec((tm, d), lambda i: (i, 0)), _const_spec((1, d))]
    in_specs += [_const_spec(w.shape) for w in ws]
    out_specs = [pl.BlockSpec((tm, w.shape[1]), lambda i: (i, 0)) for w in ws]
    out_shape = [jax.ShapeDtypeStruct((n, w.shape[1]), dt) for w, dt in zip(ws, out_dtypes)]
    return pl.pallas_call(
        functools.partial(_rms_proj_kernel, n_w=len(ws)),
        grid=(n // tm,),
        in_specs=in_specs,
        out_specs=out_specs,
        out_shape=out_shape,
        compiler_params=pltpu.CompilerParams(
            dimension_semantics=("parallel",), vmem_limit_bytes=VMEM_LIMIT_BYTES),
        name="rms_proj",
    )(x, g, *ws)


def _gdn_proj_kernel(x_ref, g_ref, w_ref, wba_ref, cw_ref, q_ref, k_ref, v_ref, z_ref, ba_ref, halo_ref,
                     *, n_heads, tiles_per_seq):
    tm = x_ref.shape[0]
    width = q_ref.shape[1]
    dh = width // n_heads
    tc = min(GDN_PROJ_COL_TILE, width)
    groups = tm // SUBLANES
    hn = (_rms_scale(x_ref[...]) * g_ref[...]).astype(BF16)

    @pl.when(lax.rem(pl.program_id(0), tiles_per_seq) == 0)
    def _():
        halo_ref[...] = jnp.zeros_like(halo_ref)

    sub = lax.broadcasted_iota(jnp.int32, (groups, SUBLANES, tc), 1)
    for sec, o_ref in enumerate((q_ref, k_ref, v_ref)):
        for c in range(width // tc):
            cols = slice(sec * width + c * tc, sec * width + (c + 1) * tc)
            y = _dot(hn, w_ref[:, cols])
            yg = jnp.concatenate([y, halo_ref[:, cols]], axis=0).reshape(groups + 1, SUBLANES, tc)
            halo_ref[:, cols] = y[tm - SUBLANES:, :]
            acc = y * cw_ref[DN_CONV - 1:DN_CONV, cols]
            for s in range(1, DN_CONV):
                r = pltpu.roll(yg, s, axis=1)
                prev = jnp.concatenate([r[groups:], r[:groups - 1]], axis=0)
                shifted = jnp.where(sub >= s, r[:groups], prev).reshape(tm, tc)
                acc = acc + shifted * cw_ref[DN_CONV - 1 - s:DN_CONV - s, cols]
            a = acc * jax.nn.sigmoid(acc)
            if sec < 2:
                parts = []
                for hh in range(tc // dh):
                    p = a[:, hh * dh:(hh + 1) * dh]
                    inv = lax.rsqrt(jnp.sum(p * p, axis=-1, keepdims=True) + L2_EPS)
                    parts.append(p * (inv * dh ** -0.5 if sec == 0 else inv))
                a = jnp.concatenate(parts, axis=1)
            o_ref[:, c * tc:(c + 1) * tc] = a.astype(o_ref.dtype)
    for c in range(width // tc):
        z_ref[:, c * tc:(c + 1) * tc] = _dot(hn, w_ref[:, 3 * width + c * tc:3 * width + (c + 1) * tc])
    ba_ref[...] = _dot(hn, wba_ref[...])


def _gdn_proj(x, g, w_main, w_ba, conv_w, seq):
    n, d = x.shape
    width = w_main.shape[1] // 4
    tm = min(ROW_TILE, seq)
    row = lambda i: (i, 0)
    return pl.pallas_call(
        functools.partial(_gdn_proj_kernel, n_heads=DN_HEADS, tiles_per_seq=seq // tm),
        grid=(n // tm,),
        in_specs=[pl.BlockSpec((tm, d), row), _const_spec((1, d)), _const_spec(w_main.shape),
                  _const_spec(w_ba.shape), _const_spec(conv_w.shape)],
        out_specs=[pl.BlockSpec((tm, width), row)] * 4 + [pl.BlockSpec((tm, LANES), row)],
        out_shape=[jax.ShapeDtypeStruct((n, width), BF16)] * 3
                  + [jax.ShapeDtypeStruct((n, width), F32), jax.ShapeDtypeStruct((n, LANES), F32)],
        scratch_shapes=[pltpu.VMEM((SUBLANES, 3 * width), F32)],
        compiler_params=pltpu.CompilerParams(
            dimension_semantics=("arbitrary",), vmem_limit_bytes=VMEM_LIMIT_BYTES),
        name="gdn_proj",
    )(x, g, w_main, w_ba, conv_w)


def _out_mlp_kernel(o_ref, h_ref, wout_ref, g_ref, wup_ref, wdn_ref, gf_ref, out_ref, *, final_norm):
    h1 = h_ref[...] + _dot(o_ref[...], wout_ref[...])
    hn = (_rms_scale(h1) * g_ref[...]).astype(BF16)
    out_ref[...] = h1
    d_ff = wup_ref.shape[1]
    tf = min(COL_TILE, d_ff)
    for c in range(d_ff // tf):
        cols = slice(c * tf, (c + 1) * tf)
        act = jnp.square(jnp.maximum(_dot(hn, wup_ref[:, cols]), 0.0)).astype(BF16)
        out_ref[...] += _dot(act, wdn_ref[cols, :])
    if final_norm:
        out_ref[...] = _rms_scale(out_ref[...]) * gf_ref[...]


def _out_mlp(o, h, w_out, g, w_up, w_down, layer, g_final, final_norm):
    n, d = h.shape
    tm = min(ROW_TILE, n)
    row = lambda i: (i, 0)
    return pl.pallas_call(
        functools.partial(_out_mlp_kernel, final_norm=final_norm),
        grid=(n // tm,),
        in_specs=[pl.BlockSpec((tm, o.shape[1]), row), pl.BlockSpec((tm, d), row),
                  _const_spec(w_out.shape), _const_spec((1, d)),
                  _layer_spec(w_up, layer), _layer_spec(w_down, layer), _const_spec((1, d))],
        out_specs=pl.BlockSpec((tm, d), row),
        out_shape=jax.ShapeDtypeStruct((n, d), F32),
        compiler_params=pltpu.CompilerParams(
            dimension_semantics=("parallel",), vmem_limit_bytes=VMEM_LIMIT_BYTES),
        name="out_mlp",
    )(o, h, w_out, g, w_up, w_down, g_final)


def _unit_lower_inverse_all(a_lows, row, col, between=()):
    c = a_lows[0].shape[0]
    eye = jnp.where(row == col, 1.0, 0.0).astype(F32)
    negs = [(-a).astype(BF16) for a in a_lows]
    between = list(between)
    ms = None
    s = 1
    while s < c:
        same_pair = (row // (2 * s)) == (col // (2 * s))
        off = same_pair & ((row % (2 * s)) >= s) & ((col % (2 * s)) < s)
        if ms is None:
            ms = [(eye - jnp.where(off, a, 0.0)).astype(BF16) for a in a_lows]
        else:
            ys = [_dot(m, n).astype(BF16) for m, n in zip(ms, negs)]
            ms = [jnp.where(off, _dot(y, m).astype(BF16), m) for y, m in zip(ys, ms)]
            if between:
                between.pop(0)()
        s *= 2
    for work in between:
        work()
    return ms


def _gdn_kernel(q_ref, k_ref, v_ref, z_ref, ba_ref, alog_ref, dtb_ref, ong_ref,
                o_ref, state_ref, *, n_heads, chunk):
    t_idx = pl.program_id(1)
    tt = q_ref.shape[0]
    dh = q_ref.shape[1] // n_heads
    n_chunks = tt // chunk

    @pl.when(t_idx == 0)
    def _():
        state_ref[...] = jnp.zeros_like(state_ref)

    ba = ba_ref[...]
    lane = lax.broadcasted_iota(jnp.int32, ba.shape, 1)
    trow = lax.broadcasted_iota(jnp.int32, ba.shape, 0)
    beta_all = jax.nn.sigmoid(ba)
    sp_in = ba + dtb_ref[...]
    softplus = jnp.maximum(sp_in, 0.0) + jnp.log(1.0 + jnp.exp(-jnp.abs(sp_in)))
    g_all = jnp.where((lane >= n_heads) & (lane < 2 * n_heads), -jnp.exp(alog_ref[...]) * softplus, 0.0)
    gc_all = g_all
    s = 1
    while s < chunk:
        shifted = pltpu.roll(gc_all, s, axis=0)
        gc_all = gc_all + jnp.where((trow % chunk) >= s, shifted, 0.0)
        s *= 2
    gc_all_t = gc_all.T
    beta_all_t = beta_all.T
    g_last_all = jnp.concatenate(
        [jnp.broadcast_to(gc_all[(c + 1) * chunk - 1:(c + 1) * chunk, :], (chunk, LANES)) for c in range(n_chunks)],
        axis=0)
    e_gc_all = jnp.exp(gc_all)
    e_rem_all = jnp.exp(g_last_all - gc_all)
    e_last_all = jnp.exp(g_last_all)

    row = lax.broadcasted_iota(jnp.int32, (chunk, chunk), 0)
    col = lax.broadcasted_iota(jnp.int32, (chunk, chunk), 1)

    rows_of = lambda c: slice(c * chunk, (c + 1) * chunk)
    lanes_of = lambda h: slice(h * dh, (h + 1) * dh)
    heads = range(n_heads)

    def recurrence_steps(c, k16, q16, attn16, sol):
        rows = rows_of(c)
        st = {}

        def read_state():
            st["states"] = [state_ref[h] for h in heads]
            st["s16"] = [s.astype(BF16) for s in st["states"]]
            st["ws"] = [_dot(sol[h][:, dh:].astype(BF16), st["s16"][h]) for h in heads]

        def new_values():
            st["v_new"] = [sol[h][:, :dh] - st["ws"][h] for h in heads]
            vn16 = [v.astype(BF16) for v in st["v_new"]]
            st["o_inter"] = [_dot(q16[h], st["s16"][h]) for h in heads]
            st["o_intra"] = [_dot(attn16[h], vn16[h]) for h in heads]
            st["s_upd"] = [_dot_tn(k16[h], (st["v_new"][h] * e_rem_all[rows, n_heads + h:n_heads + h + 1]
                                           ).astype(BF16)) for h in heads]

        def write_out():
            for h in heads:
                gl = n_heads + h
                state_ref[h] = st["states"][h] * e_last_all[c * chunk:c * chunk + 1, gl:gl + 1] + st["s_upd"][h]
                z = z_ref[rows, lanes_of(h)]
                o = st["o_inter"][h] * e_gc_all[rows, gl:gl + 1] + st["o_intra"][h]
                o = _rms_scale(o) * ong_ref[...] * (z * jax.nn.sigmoid(z))
                o_ref[rows, lanes_of(h)] = o.astype(o_ref.dtype)

        return [read_state, new_values, write_out]

    group_chunks = [list(range(c0, min(c0 + GDN_GROUP_CHUNKS, n_chunks)))
                    for c0 in range(0, n_chunks, GDN_GROUP_CHUNKS)]
    group_pairs = [[(c, h) for c in cs for h in heads] for cs in group_chunks]
    k16 = [[k_ref[rows_of(c), lanes_of(h)] for c, h in pairs] for pairs in group_pairs]
    q16 = [[q_ref[rows_of(c), lanes_of(h)] for c, h in pairs] for pairs in group_pairs]
    kk = [[_dot_nt(k, k) for k in ks] for ks in k16]
    qk = [[_dot_nt(q, k) for q, k in zip(qs, ks)] for qs, ks in zip(q16, k16)]
    pending = []
    for g, pairs in enumerate(group_pairs):
        a_low, attn16 = [], []
        for (c, h), kk_p, qk_p in zip(pairs, kk[g], qk[g]):
            gl = n_heads + h
            gc = gc_all[rows_of(c), gl:gl + 1]
            gc_row = gc_all_t[gl:gl + 1, rows_of(c)]
            decay = jnp.exp(jnp.where(row >= col, gc - gc_row, -jnp.inf))
            a_low.append(jnp.where(row > col, kk_p * (decay * beta_all[rows_of(c), h:h + 1]), 0.0))
            attn16.append((qk_p * decay).astype(BF16))
        t_inv = _unit_lower_inverse_all(a_low, row, col, between=pending)
        sol = []
        for (c, h), t, k in zip(pairs, t_inv, k16[g]):
            gl = n_heads + h
            t_beta = (t.astype(F32) * beta_all_t[h:h + 1, rows_of(c)]).astype(BF16)
            k_dec = (k.astype(F32) * e_gc_all[rows_of(c), gl:gl + 1]).astype(BF16)
            sol.append(_dot(t_beta, jnp.concatenate([v_ref[rows_of(c), lanes_of(h)], k_dec], axis=-1)))
        pending = []
        for ci, c in enumerate(group_chunks[g]):
            part = slice(ci * n_heads, (ci + 1) * n_heads)
            pending += recurrence_steps(c, k16[g][part], q16[g][part], attn16[part], sol[part])
    for work in pending:
        work()


def _gdn_core(q, k, v, z, ba, a_log_row, dt_bias_row, out_norm_g, batch, seq):
    n_heads, dh = DN_HEADS, DN_HEAD_DIM
    width = n_heads * dh
    tt = min(GDN_TIME_TILE, seq)
    nt = seq // tt
    tile = lambda b, t: (b * nt + t, 0)
    return pl.pallas_call(
        functools.partial(_gdn_kernel, n_heads=n_heads, chunk=min(GDN_CHUNK, tt)),
        grid=(batch, nt),
        in_specs=[pl.BlockSpec((tt, width), tile)] * 4 + [pl.BlockSpec((tt, LANES), tile),
                  _const_spec((1, LANES)), _const_spec((1, LANES)), _const_spec((1, dh))],
        out_specs=pl.BlockSpec((tt, width), tile),
        out_shape=jax.ShapeDtypeStruct((batch * seq, width), BF16),
        scratch_shapes=[pltpu.VMEM((n_heads, dh, dh), F32)],
        compiler_params=pltpu.CompilerParams(
            dimension_semantics=("parallel", "arbitrary"), vmem_limit_bytes=VMEM_LIMIT_BYTES),
        name="gdn_core",
    )(q, k, v, z, ba, a_log_row, dt_bias_row, out_norm_g)


def _split3_f32(x):
    hi = x.astype(BF16).astype(F32)
    mid = (x - hi).astype(BF16).astype(F32)
    return hi, mid, ((x - hi) - mid).astype(BF16).astype(F32)


def _moba_kernel(q_ref, k_ref, v_ref, o_ref, kmean_ref, vt_ref, maskb_ref, kaug_ref, qaug_ref,
                 m_ref, acc_ref, *, n_heads, top_k):
    i = pl.program_id(1)
    blk = q_ref.shape[0]
    seq = k_ref.shape[0]
    dh = k_ref.shape[1] // n_heads
    nb = seq // blk
    scale2 = dh ** -0.5 * math.log2(math.e)
    heads = range(n_heads)
    lanes = [slice(h * dh, (h + 1) * dh) for h in heads]
    slope2 = [2.0 ** (-8.0 * (h + 1) / n_heads) * math.log2(math.e) for h in heads]

    krow = lax.broadcasted_iota(jnp.int32, (blk, blk), 0)
    qcol = lax.broadcasted_iota(jnp.int32, (blk, blk), 1)
    aug_lane = lax.broadcasted_iota(jnp.int32, (blk, dh), 1)

    @pl.when(i == 0)
    def _():
        r = lax.broadcasted_iota(jnp.int32, (nb, seq), 0)
        c = lax.broadcasted_iota(jnp.int32, (nb, seq), 1)
        ind = jnp.where(c // blk == r, 1.0, 0.0).astype(BF16)
        key_pos = lax.broadcasted_iota(jnp.int32, (blk, dh), 0).astype(F32)
        for h in heads:
            kmean_ref[h] = (_dot(ind, k_ref[:, lanes[h]]) * (1.0 / blk)).astype(BF16)
            hi, mid, lo = _split3_f32(slope2[h] * key_pos)
            kaug_ref[h] = jnp.where(aug_lane == 0, hi, jnp.where(aug_lane == 1, mid,
                                                                 jnp.where(aug_lane == 2, lo, 0.0))).astype(BF16)
            for j in range(nb):
                vt = v_ref[j * blk:(j + 1) * blk, lanes[h]].astype(F32).T.astype(BF16)
                vt_ref[h, j] = jnp.concatenate([vt, jnp.ones((MOBA_SUM_ROWS, blk), BF16)], axis=0)

    qs = [q_ref[:, lanes[h]] for h in heads]
    ones_cols = jnp.where(aug_lane < 3, 1.0, 0.0).astype(BF16)
    for h in heads:
        qaug_ref[h] = jnp.concatenate([(qs[h].astype(F32) * scale2).astype(BF16), ones_cols], axis=1)

    gates = [_dot_nt(kmean_ref[h], qs[h]) for h in heads]
    bidx = lax.broadcasted_iota(jnp.int32, (nb, blk), 0)
    past = bidx < i
    for h in heads:
        gate = jnp.where(past, gates[h], -jnp.inf)
        rank = jnp.zeros(gate.shape, F32)
        for jp in range(nb):
            other = gate[jp:jp + 1, :]
            beats = (other > gate) | ((other == gate) & (jp < bidx))
            rank = rank + jnp.where(beats, 1.0, 0.0)
        maskb_ref[h] = jnp.where(past & (rank < top_k), 0.0, -jnp.inf)

    def pipelined(tiles, score_mm, rest):
        raw = {}
        for step in range(len(tiles) + MOBA_SCORE_LEAD):
            if step < len(tiles):
                raw[step] = score_mm(*tiles[step])
            if step >= MOBA_SCORE_LEAD:
                rest(*tiles[step - MOBA_SCORE_LEAD], raw.pop(step - MOBA_SCORE_LEAD))

    def scores(h, j):
        rows = pl.ds(pl.multiple_of(j * blk, blk), blk)
        k_aug = jnp.concatenate([k_ref[rows, lanes[h]], kaug_ref[h]], axis=1)
        return _dot_nt(k_aug, qaug_ref[h])

    def own_rest(h, raw):
        s = jnp.where(krow <= qcol, raw, -jnp.inf)
        m = jnp.max(s, axis=0, keepdims=True)
        m_ref[h] = m
        acc_ref[h] = _dot(vt_ref[h, i], jnp.exp2(s - m).astype(BF16))

    def past_rest(h, j, raw):
        row_bias = maskb_ref[h, pl.ds(j, 1), :] + slope2[h] * ((j - i) * blk).astype(F32)
        m_old = m_ref[h]
        m_new = jnp.maximum(m_old, jnp.max(raw, axis=0, keepdims=True) + row_bias)
        m_ref[h] = m_new
        p = jnp.exp2(raw + (row_bias - m_new)).astype(BF16)
        acc_ref[h] = acc_ref[h] * jnp.exp2(m_old - m_new) + _dot(vt_ref[h, j], p)

    def blocks(js, with_own):
        tiles = ([(h, None) for h in heads] if with_own else []) + [(h, j) for j in js for h in heads]
        pipelined(tiles, lambda h, j: scores(h, i if j is None else j),
                  lambda h, j, raw: own_rest(h, raw) if j is None else past_rest(h, j, raw))

    quads = i >> 2
    for rem in range(4):
        @pl.when((i & 3) == rem)
        def _():
            blocks([quads * 4 + d for d in range(rem)], with_own=True)

    def quad_body(jj, carry):
        blocks([4 * jj + d for d in range(4)], with_own=False)
        return carry

    lax.fori_loop(0, quads, quad_body, 0)

    for h in heads:
        acc = acc_ref[h]
        o_ref[:, lanes[h]] = (acc[:dh] / acc[dh:dh + 1]).T.astype(o_ref.dtype)


def _moba_attn(qkv, batch, seq):
    n_heads = MB_HEADS
    width = qkv.shape[1] // 3
    dh = width // n_heads
    blk = min(MB_BLOCK, seq)
    nb = seq // blk
    return pl.pallas_call(
        functools.partial(_moba_kernel, n_heads=n_heads, top_k=min(MB_TOPK, nb)),
        grid=(batch, nb),
        in_specs=[pl.BlockSpec((blk, width), lambda b, i: (b * nb + i, 0)),
                  pl.BlockSpec((seq, width), lambda b, i: (b, 1)),
                  pl.BlockSpec((seq, width), lambda b, i: (b, 2))],
        out_specs=pl.BlockSpec((blk, width), lambda b, i: (b * nb + i, 0)),
        out_shape=jax.ShapeDtypeStruct((batch * seq, width), BF16),
        scratch_shapes=[pltpu.VMEM((n_heads, nb, dh), BF16),
                        pltpu.VMEM((n_heads, nb, dh + MOBA_SUM_ROWS, blk), BF16),
                        pltpu.VMEM((n_heads, nb, blk), F32),
                        pltpu.VMEM((n_heads, blk, dh), BF16),
                        pltpu.VMEM((n_heads, blk, 2 * dh), BF16),
                        pltpu.VMEM((n_heads, 1, blk), F32),
                        pltpu.VMEM((n_heads, dh + MOBA_SUM_ROWS, blk), F32)],
        compiler_params=pltpu.CompilerParams(
            dimension_semantics=("parallel", "arbitrary"), vmem_limit_bytes=VMEM_LIMIT_BYTES),
        name="moba_attn",
    )(qkv, qkv, qkv)


def _pad_lanes(row_vec, offset):
    return jnp.zeros((1, LANES), F32).at[0, offset:offset + row_vec.shape[0]].set(row_vec.astype(F32))


def kernel(x, mix_norm_g, mlp_norm_g, dn_w_in, dn_conv_w, dn_a_log, dn_dt_bias, dn_out_norm_g, dn_w_out,
           mb_w_in, mb_w_out, mlp_w_up, mlp_w_down, final_norm_g):
    batch, seq, d = x.shape
    n = batch * seq
    h = x.reshape(n, d)
    dn_dim = DN_HEADS * DN_HEAD_DIM
    row = lambda v: v.reshape(1, -1).astype(F32)
    w_up, w_down = mlp_w_up.astype(BF16), mlp_w_down.astype(BF16)

    w_main = dn_w_in[0, :, :4 * dn_dim].astype(BF16)
    w_ba = jnp.zeros((d, LANES), F32).at[:, :2 * DN_HEADS].set(dn_w_in[0, :, 4 * dn_dim:]).astype(BF16)
    q, k, v, z, ba = _gdn_proj(h, row(mix_norm_g[0]), w_main, w_ba, dn_conv_w[0].astype(F32), seq)
    o = _gdn_core(q, k, v, z, ba, _pad_lanes(dn_a_log[0], DN_HEADS), _pad_lanes(dn_dt_bias[0], DN_HEADS),
                  row(dn_out_norm_g[0]), batch, seq)
    h = _out_mlp(o, h, dn_w_out[0].astype(BF16), row(mlp_norm_g[0]), w_up, w_down, 0,
                 row(final_norm_g), final_norm=False)

    (qkv,) = _rms_proj(h, row(mix_norm_g[1]), [mb_w_in[0].astype(BF16)], [BF16])
    o = _moba_attn(qkv, batch, seq)
    h = _out_mlp(o, h, mb_w_out[0].astype(BF16), row(mlp_norm_g[1]), w_up, w_down, 1,
                 row(final_norm_g), final_norm=True)
    return h.reshape(batch, seq, d)
```

```python
import functools
import math

import jax
import jax.numpy as jnp
from jax import lax
from jax.experimental import pallas as pl
from jax.experimental.pallas import tpu as pltpu

F32 = jnp.float32
BF16 = jnp.bfloat16

RMS_EPS = 1e-6
L2_EPS = 1e-6

DN_HEADS = 8
DN_HEAD_DIM = 128
DN_CONV = 4
GDN_CHUNK = 128
MB_HEADS = 8
MB_BLOCK = 256
MB_TOPK = 3

LANES = 128
SUBLANES = 8
VMEM_LIMIT_BYTES = 56 * 1024 * 1024

ROW_TILE = 512
COL_TILE = 512
GDN_PROJ_COL_TILE = 256
GDN_TIME_TILE = 512
GDN_GROUP_CHUNKS = 2
MOBA_SUM_ROWS = 16
MOBA_SCORE_LEAD = 4


def _dot(a, b):
    return jnp.dot(a, b, preferred_element_type=F32)


def _dot_nt(a, b):
    return lax.dot_general(a, b, (((1,), (1,)), ((), ())), preferred_element_type=F32)


def _dot_tn(a, b):
    return lax.dot_general(a, b, (((0,), (0,)), ((), ())), preferred_element_type=F32)


def _rms_scale(x):
    return x * lax.rsqrt(jnp.mean(x * x, axis=-1, keepdims=True) + RMS_EPS)


def _const_spec(shape):
    return pl.BlockSpec(shape, lambda *_: (0,) * len(shape), pipeline_mode=pl.Buffered(1))


def _layer_spec(stacked, layer):
    return pl.BlockSpec((None,) + stacked.shape[1:], lambda *_: (layer, 0, 0), pipeline_mode=pl.Buffered(1))


def _rms_proj_kernel(x_ref, g_ref, *refs, n_w):
    w_refs, o_refs = refs[:n_w], refs[n_w:]
    hn = (_rms_scale(x_ref[...]) * g_ref[...]).astype(BF16)
    for w_ref, o_ref in zip(w_refs, o_refs):
        n_cols = w_ref.shape[1]
        tn = min(COL_TILE, n_cols)
        for c in range(n_cols // tn):
            cols = slice(c * tn, (c + 1) * tn)
            o_ref[:, cols] = _dot(hn, w_ref[:, cols]).astype(o_ref.dtype)


def _rms_proj(x, g, ws, out_dtypes):
    n, d = x.shape
    tm = min(ROW_TILE, n)
    in_specs = [pl.BlockSpec((tm, d), lambda i: (i, 0)), _const_spec((1, d))]
    in_specs += [_const_spec(w.shape) for w in ws]
    out_specs = [pl.BlockSpec((tm, w.shape[1]), lambda i: (i, 0)) for w in ws]
    out_shape = [jax.ShapeDtypeStruct((n, w.shape[1]), dt) for w, dt in zip(ws, out_dtypes)]
    return pl.pallas_call(
        functools.partial(_rms_proj_kernel, n_w=len(ws)),
        grid=(n // tm,),
        in_specs=in_specs,
        out_specs=out_specs,
        out_shape=out_shape,
        compiler_params=pltpu.CompilerParams(
            dimension_semantics=("parallel",), vmem_limit_bytes=VMEM_LIMIT_BYTES),
        name="rms_proj",
    )(x, g, *ws)


def _gdn_proj_kernel(x_ref, g_ref, w_ref, wba_ref, cw_ref, q_ref, k_ref, v_ref, z_ref, ba_ref, halo_ref,
                     *, n_heads, tiles_per_seq):
    tm = x_ref.shape[0]
    width = q_ref.shape[1]
    dh = width // n_heads
    tc = min(GDN_PROJ_COL_TILE, width)
    groups = tm // SUBLANES
    hn = (_rms_scale(x_ref[...]) * g_ref[...]).astype(BF16)

    @pl.when(lax.rem(pl.program_id(0), tiles_per_seq) == 0)
    def _():
        halo_ref[...] = jnp.zeros_like(halo_ref)

    sub = lax.broadcasted_iota(jnp.int32, (groups, SUBLANES, tc), 1)
    for sec, o_ref in enumerate((q_ref, k_ref, v_ref)):
        for c in range(width // tc):
            cols = slice(sec * width + c * tc, sec * width + (c + 1) * tc)
            y = _dot(hn, w_ref[:, cols])
            yg = jnp.concatenate([y, halo_ref[:, cols]], axis=0).reshape(groups + 1, SUBLANES, tc)
            halo_ref[:, cols] = y[tm - SUBLANES:, :]
            acc = y * cw_ref[DN_CONV - 1:DN_CONV, cols]
            for s in range(1, DN_CONV):
                r = pltpu.roll(yg, s, axis=1)
                prev = jnp.concatenate([r[groups:], r[:groups - 1]], axis=0)
                shifted = jnp.where(sub >= s, r[:groups], prev).reshape(tm, tc)
                acc = acc + shifted * cw_ref[DN_CONV - 1 - s:DN_CONV - s, cols]
            a = acc * jax.nn.sigmoid(acc)
            if sec < 2:
                parts = []
                for hh in range(tc // dh):
                    p = a[:, hh * dh:(hh + 1) * dh]
                    inv = lax.rsqrt(jnp.sum(p * p, axis=-1, keepdims=True) + L2_EPS)
                    parts.append(p * (inv * dh ** -0.5 if sec == 0 else inv))
                a = jnp.concatenate(parts, axis=1)
            o_ref[:, c * tc:(c + 1) * tc] = a.astype(o_ref.dtype)
    for c in range(width // tc):
        z_ref[:, c * tc:(c + 1) * tc] = _dot(hn, w_ref[:, 3 * width + c * tc:3 * width + (c + 1) * tc])
    ba_ref[...] = _dot(hn, wba_ref[...])


def _gdn_proj(x, g, w_main, w_ba, conv_w, seq):
    n, d = x.shape
    width = w_main.shape[1] // 4
    tm = min(ROW_TILE, seq)
    row = lambda i: (i, 0)
    return pl.pallas_call(
        functools.partial(_gdn_proj_kernel, n_heads=DN_HEADS, tiles_per_seq=seq // tm),
        grid=(n // tm,),
        in_specs=[pl.BlockSpec((tm, d), row), _const_spec((1, d)), _const_spec(w_main.shape),
                  _const_spec(w_ba.shape), _const_spec(conv_w.shape)],
        out_specs=[pl.BlockSpec((tm, width), row)] * 4 + [pl.BlockSpec((tm, LANES), row)],
        out_shape=[jax.ShapeDtypeStruct((n, width), BF16)] * 3
                  + [jax.ShapeDtypeStruct((n, width), F32), jax.ShapeDtypeStruct((n, LANES), F32)],
        scratch_shapes=[pltpu.VMEM((SUBLANES, 3 * width), F32)],
        compiler_params=pltpu.CompilerParams(
            dimension_semantics=("arbitrary",), vmem_limit_bytes=VMEM_LIMIT_BYTES),
        name="gdn_proj",
    )(x, g, w_main, w_ba, conv_w)


def _out_mlp_kernel(o_ref, h_ref, wout_ref, g_ref, wup_ref, wdn_ref, gf_ref, out_ref, *, final_norm):
    h1 = h_ref[...] + _dot(o_ref[...], wout_ref[...])
    hn = (_rms_scale(h1) * g_ref[...]).astype(BF16)
    out_ref[...] = h1
    d_ff = wup_ref.shape[1]
    tf = min(COL_TILE, d_ff)
    for c in range(d_ff // tf):
        cols = slice(c * tf, (c + 1) * tf)
        act = jnp.square(jnp.maximum(_dot(hn, wup_ref[:, cols]), 0.0)).astype(BF16)
        out_ref[...] += _dot(act, wdn_ref[cols, :])
    if final_norm:
        out_ref[...] = _rms_scale(out_ref[...]) * gf_ref[...]


def _out_mlp(o, h, w_out, g, w_up, w_down, layer, g_final, final_norm):
    n, d = h.shape
    tm = min(ROW_TILE, n)
    row = lambda i: (i, 0)
    return pl.pallas_call(
        functools.partial(_out_mlp_kernel, final_norm=final_norm),
        grid=(n // tm,),
        in_specs=[pl.BlockSpec((tm, o.shape[1]), row), pl.BlockSpec((tm, d), row),
                  _const_spec(w_out.shape), _const_spec((1, d)),
                  _layer_spec(w_up, layer), _layer_spec(w_down, layer), _const_spec((1, d))],
        out_specs=pl.BlockSpec((tm, d), row),
        out_shape=jax.ShapeDtypeStruct((n, d), F32),
        compiler_params=pltpu.CompilerParams(
            dimension_semantics=("parallel",), vmem_limit_bytes=VMEM_LIMIT_BYTES),
        name="out_mlp",
    )(o, h, w_out, g, w_up, w_down, g_final)


def _unit_lower_inverse_all(a_lows, row, col, between=()):
    c = a_lows[0].shape[0]
    eye = jnp.where(row == col, 1.0, 0.0).astype(F32)
    negs = [(-a).astype(BF16) for a in a_lows]
    between = list(between)
    ms = None
    s = 1
    while s < c:
        same_pair = (row // (2 * s)) == (col // (2 * s))
        off = same_pair & ((row % (2 * s)) >= s) & ((col % (2 * s)) < s)
        if ms is None:
            ms = [(eye - jnp.where(off, a, 0.0)).astype(BF16) for a in a_lows]
        else:
            ys = [_dot(m, n).astype(BF16) for m, n in zip(ms, negs)]
            ms = [jnp.where(off, _dot(y, m).astype(BF16), m) for y, m in zip(ys, ms)]
            if between:
                between.pop(0)()
        s *= 2
    for work in between:
        work()
    return ms


def _gdn_kernel(q_ref, k_ref, v_ref, z_ref, ba_ref, alog_ref, dtb_ref, ong_ref,
                o_ref, state_ref, *, n_heads, chunk):
    t_idx = pl.program_id(1)
    tt = q_ref.shape[0]
    dh = q_ref.shape[1] // n_heads
    n_chunks = tt // chunk

    @pl.when(t_idx == 0)
    def _():
        state_ref[...] = jnp.zeros_like(state_ref)

    ba = ba_ref[...]
    lane = lax.broadcasted_iota(jnp.int32, ba.shape, 1)
    trow = lax.broadcasted_iota(jnp.int32, ba.shape, 0)
    beta_all = jax.nn.sigmoid(ba)
    sp_in = ba + dtb_ref[...]
    softplus = jnp.maximum(sp_in, 0.0) + jnp.log(1.0 + jnp.exp(-jnp.abs(sp_in)))
    g_all = jnp.where((lane >= n_heads) & (lane < 2 * n_heads), -jnp.exp(alog_ref[...]) * softplus, 0.0)
    gc_all = g_all
    s = 1
    while s < chunk:
        shifted = pltpu.roll(gc_all, s, axis=0)
        gc_all = gc_all + jnp.where((trow % chunk) >= s, shifted, 0.0)
        s *= 2
    gc_all_t = gc_all.T
    beta_all_t = beta_all.T
    g_last_all = jnp.concatenate(
        [jnp.broadcast_to(gc_all[(c + 1) * chunk - 1:(c + 1) * chunk, :], (chunk, LANES)) for c in range(n_chunks)],
        axis=0)
    e_gc_all = jnp.exp(gc_all)
    e_rem_all = jnp.exp(g_last_all - gc_all)
    e_last_all = jnp.exp(g_last_all)

    row = lax.broadcasted_iota(jnp.int32, (chunk, chunk), 0)
    col = lax.broadcasted_iota(jnp.int32, (chunk, chunk), 1)

    rows_of = lambda c: slice(c * chunk, (c + 1) * chunk)
    lanes_of = lambda h: slice(h * dh, (h + 1) * dh)
    heads = range(n_heads)

    def recurrence_steps(c, k16, q16, attn16, sol):
        rows = rows_of(c)
        st = {}

        def read_state():
            st["states"] = [state_ref[h] for h in heads]
            st["s16"] = [s.astype(BF16) for s in st["states"]]
            st["wq"] = [_dot(jnp.concatenate([sol[h][:, dh:].astype(BF16), q16[h]], axis=0), st["s16"][h])
                        for h in heads]

        def new_values():
            st["v_new"] = [sol[h][:, :dh] - st["wq"][h][:chunk] for h in heads]
            vn16 = [v.astype(BF16) for v in st["v_new"]]
            st["o_inter"] = [wq[chunk:] for wq in st["wq"]]
            st["o_intra"] = [_dot(attn16[h], vn16[h]) for h in heads]
            st["s_upd"] = [_dot_tn(k16[h], (st["v_new"][h] * e_rem_all[rows, n_heads + h:n_heads + h + 1]
                                           ).astype(BF16)) for h in heads]

        def write_out():
            for h in heads:
                gl = n_heads + h
                state_ref[h] = st["states"][h] * e_last_all[c * chunk:c * chunk + 1, gl:gl + 1] + st["s_upd"][h]
                z = z_ref[rows, lanes_of(h)]
                o = st["o_inter"][h] * e_gc_all[rows, gl:gl + 1] + st["o_intra"][h]
                o = _rms_scale(o) * ong_ref[...] * (z * jax.nn.sigmoid(z))
                o_ref[rows, lanes_of(h)] = o.astype(o_ref.dtype)

        return [read_state, new_values, write_out]

    group_chunks = [list(range(c0, min(c0 + GDN_GROUP_CHUNKS, n_chunks)))
                    for c0 in range(0, n_chunks, GDN_GROUP_CHUNKS)]
    group_pairs = [[(c, h) for c in cs for h in heads] for cs in group_chunks]
    k16 = [[k_ref[rows_of(c), lanes_of(h)] for c, h in pairs] for pairs in group_pairs]
    q16 = [[q_ref[rows_of(c), lanes_of(h)] for c, h in pairs] for pairs in group_pairs]
    kq = [[_dot_nt(jnp.concatenate([k, q], axis=0), k) for q, k in zip(qs, ks)] for qs, ks in zip(q16, k16)]
    pending = []
    for g, pairs in enumerate(group_pairs):
        a_low, attn16 = [], []
        for (c, h), kq_p in zip(pairs, kq[g]):
            kk_p, qk_p = kq_p[:chunk], kq_p[chunk:]
            gl = n_heads + h
            gc = gc_all[rows_of(c), gl:gl + 1]
            gc_row = gc_all_t[gl:gl + 1, rows_of(c)]
            decay = jnp.exp(jnp.where(row >= col, gc - gc_row, -jnp.inf))
            a_low.append(jnp.where(row > col, kk_p * (decay * beta_all[rows_of(c), h:h + 1]), 0.0))
            attn16.append((qk_p * decay).astype(BF16))
        t_inv = _unit_lower_inverse_all(a_low, row, col, between=pending)
        sol = []
        for (c, h), t, k in zip(pairs, t_inv, k16[g]):
            gl = n_heads + h
            t_beta = (t.astype(F32) * beta_all_t[h:h + 1, rows_of(c)]).astype(BF16)
            k_dec = (k.astype(F32) * e_gc_all[rows_of(c), gl:gl + 1]).astype(BF16)
            sol.append(_dot(t_beta, jnp.concatenate([v_ref[rows_of(c), lanes_of(h)], k_dec], axis=-1)))
        pending = []
        for ci, c in enumerate(group_chunks[g]):
            part = slice(ci * n_heads, (ci + 1) * n_heads)
            pending += recurrence_steps(c, k16[g][part], q16[g][part], attn16[part], sol[part])
    for work in pending:
        work()


def _gdn_core(q, k, v, z, ba, a_log_row, dt_bias_row, out_norm_g, batch, seq):
    n_heads, dh = DN_HEADS, DN_HEAD_DIM
    width = n_heads * dh
    tt = min(GDN_TIME_TILE, seq)
    nt = seq // tt
    tile = lambda b, t: (b * nt + t, 0)
    return pl.pallas_call(
        functools.partial(_gdn_kernel, n_heads=n_heads, chunk=min(GDN_CHUNK, tt)),
        grid=(batch, nt),
        in_specs=[pl.BlockSpec((tt, width), tile)] * 4 + [pl.BlockSpec((tt, LANES), tile),
                  _const_spec((1, LANES)), _const_spec((1, LANES)), _const_spec((1, dh))],
        out_specs=pl.BlockSpec((tt, width), tile),
        out_shape=jax.ShapeDtypeStruct((batch * seq, width), BF16),
        scratch_shapes=[pltpu.VMEM((n_heads, dh, dh), F32)],
        compiler_params=pltpu.CompilerParams(
            dimension_semantics=("parallel", "arbitrary"), vmem_limit_bytes=VMEM_LIMIT_BYTES),
        name="gdn_core",
    )(q, k, v, z, ba, a_log_row, dt_bias_row, out_norm_g)


def _split3_f32(x):
    hi = x.astype(BF16).astype(F32)
    mid = (x - hi).astype(BF16).astype(F32)
    return hi, mid, ((x - hi) - mid).astype(BF16).astype(F32)


def _moba_kernel(q_ref, k_ref, v_ref, o_ref, kmean_ref, vt_ref, maskb_ref, kaug_ref, qaug_ref,
                 m_ref, acc_ref, *, n_heads, top_k):
    i = pl.program_id(1)
    blk = q_ref.shape[0]
    seq = k_ref.shape[0]
    dh = k_ref.shape[1] // n_heads
    nb = seq // blk
    scale2 = dh ** -0.5 * math.log2(math.e)
    heads = range(n_heads)
    lanes = [slice(h * dh, (h + 1) * dh) for h in heads]
    slope2 = [2.0 ** (-8.0 * (h + 1) / n_heads) * math.log2(math.e) for h in heads]

    krow = lax.broadcasted_iota(jnp.int32, (blk, blk), 0)
    qcol = lax.broadcasted_iota(jnp.int32, (blk, blk), 1)
    aug_lane = lax.broadcasted_iota(jnp.int32, (blk, dh), 1)

    @pl.when(i == 0)
    def _():
        r = lax.broadcasted_iota(jnp.int32, (nb, seq), 0)
        c = lax.broadcasted_iota(jnp.int32, (nb, seq), 1)
        ind = jnp.where(c // blk == r, 1.0, 0.0).astype(BF16)
        key_pos = lax.broadcasted_iota(jnp.int32, (blk, dh), 0).astype(F32)
        for h in heads:
            kmean_ref[h] = (_dot(ind, k_ref[:, lanes[h]]) * (1.0 / blk)).astype(BF16)
            hi, mid, lo = _split3_f32(slope2[h] * key_pos)
            kaug_ref[h] = jnp.where(aug_lane == 0, hi, jnp.where(aug_lane == 1, mid,
                                                                 jnp.where(aug_lane == 2, lo, 0.0))).astype(BF16)
            for j in range(nb):
                vt = v_ref[j * blk:(j + 1) * blk, lanes[h]].astype(F32).T.astype(BF16)
                vt_ref[h, j] = jnp.concatenate([vt, jnp.ones((MOBA_SUM_ROWS, blk), BF16)], axis=0)

    qs = [q_ref[:, lanes[h]] for h in heads]
    ones_cols = jnp.where(aug_lane < 3, 1.0, 0.0).astype(BF16)
    for h in heads:
        qaug_ref[h] = jnp.concatenate([(qs[h].astype(F32) * scale2).astype(BF16), ones_cols], axis=1)

    gates = [_dot_nt(kmean_ref[h], qs[h]) for h in heads]
    bidx = lax.broadcasted_iota(jnp.int32, (nb, blk), 0)
    past = bidx < i
    for h in heads:
        gate = jnp.where(past, gates[h], -jnp.inf)
        rank = jnp.zeros(gate.shape, F32)
        for jp in range(nb):
            other = gate[jp:jp + 1, :]
            beats = (other > gate) | ((other == gate) & (jp < bidx))
            rank = rank + jnp.where(beats, 1.0, 0.0)
        maskb_ref[h] = jnp.where(past & (rank < top_k), 0.0, -jnp.inf)

    def pipelined(tiles, score_mm, rest):
        raw = {}
        for step in range(len(tiles) + MOBA_SCORE_LEAD):
            if step < len(tiles):
                raw[step] = score_mm(*tiles[step])
            if step >= MOBA_SCORE_LEAD:
                rest(*tiles[step - MOBA_SCORE_LEAD], raw.pop(step - MOBA_SCORE_LEAD))

    def scores(h, j):
        rows = pl.ds(pl.multiple_of(j * blk, blk), blk)
        k_aug = jnp.concatenate([k_ref[rows, lanes[h]], kaug_ref[h]], axis=1)
        return _dot_nt(k_aug, qaug_ref[h])

    def own_rest(h, raw):
        s = jnp.where(krow <= qcol, raw, -jnp.inf)
        m = jnp.max(s, axis=0, keepdims=True)
        m_ref[h] = m
        acc_ref[h] = _dot(vt_ref[h, i], jnp.exp2(s - m).astype(BF16))

    def past_rest(h, j, raw):
        row_bias = maskb_ref[h, pl.ds(j, 1), :] + slope2[h] * ((j - i) * blk).astype(F32)
        m_old = m_ref[h]
        m_new = jnp.maximum(m_old, jnp.max(raw, axis=0, keepdims=True) + row_bias)
        m_ref[h] = m_new
        p = jnp.exp2(raw + (row_bias - m_new)).astype(BF16)
        acc_ref[h] = acc_ref[h] * jnp.exp2(m_old - m_new) + _dot(vt_ref[h, j], p)

    def blocks(js, with_own):
        tiles = ([(h, None) for h in heads] if with_own else []) + [(h, j) for j in js for h in heads]
        pipelined(tiles, lambda h, j: scores(h, i if j is None else j),
                  lambda h, j, raw: own_rest(h, raw) if j is None else past_rest(h, j, raw))

    quads = i >> 2
    for rem in range(4):
        @pl.when((i & 3) == rem)
        def _():
            blocks([quads * 4 + d for d in range(rem)], with_own=True)

    def quad_body(jj, carry):
        blocks([4 * jj + d for d in range(4)], with_own=False)
        return carry

    lax.fori_loop(0, quads, quad_body, 0)

    for h in heads:
        acc = acc_ref[h]
        o_ref[:, lanes[h]] = (acc[:dh] / acc[dh:dh + 1]).T.astype(o_ref.dtype)


def _moba_attn(qkv, batch, seq):
    n_heads = MB_HEADS
    width = qkv.shape[1] // 3
    dh = width // n_heads
    blk = min(MB_BLOCK, seq)
    nb = seq // blk
    return pl.pallas_call(
        functools.partial(_moba_kernel, n_heads=n_heads, top_k=min(MB_TOPK, nb)),
        grid=(batch, nb),
        in_specs=[pl.BlockSpec((blk, width), lambda b, i: (b * nb + i, 0)),
                  pl.BlockSpec((seq, width), lambda b, i: (b, 1)),
                  pl.BlockSpec((seq, width), lambda b, i: (b, 2))],
        out_specs=pl.BlockSpec((blk, width), lambda b, i: (b * nb + i, 0)),
        out_shape=jax.ShapeDtypeStruct((batch * seq, width), BF16),
        scratch_shapes=[pltpu.VMEM((n_heads, nb, dh), BF16),
                        pltpu.VMEM((n_heads, nb, dh + MOBA_SUM_ROWS, blk), BF16),
                        pltpu.VMEM((n_heads, nb, blk), F32),
                        pltpu.VMEM((n_heads, blk, dh), BF16),
                        pltpu.VMEM((n_heads, blk, 2 * dh), BF16),
                        pltpu.VMEM((n_heads, 1, blk), F32),
                        pltpu.VMEM((n_heads, dh + MOBA_SUM_ROWS, blk), F32)],
        compiler_params=pltpu.CompilerParams(
            dimension_semantics=("parallel", "arbitrary"), vmem_limit_bytes=VMEM_LIMIT_BYTES),
        name="moba_attn",
    )(qkv, qkv, qkv)


def _pad_lanes(row_vec, offset):
    return jnp.zeros((1, LANES), F32).at[0, offset:offset + row_vec.shape[0]].set(row_vec.astype(F32))


def kernel(x, mix_norm_g, mlp_norm_g, dn_w_in, dn_conv_w, dn_a_log, dn_dt_bias, dn_out_norm_g, dn_w_out,
           mb_w_in, mb_w_out, mlp_w_up, mlp_w_down, final_norm_g):
    batch, seq, d = x.shape
    n = batch * seq
    h = x.reshape(n, d)
    dn_dim = DN_HEADS * DN_HEAD_DIM
    row = lambda v: v.reshape(1, -1).astype(F32)
    w_up, w_down = mlp_w_up.astype(BF16), mlp_w_down.astype(BF16)

    w_main = dn_w_in[0, :, :4 * dn_dim].astype(BF16)
    w_ba = jnp.zeros((d, LANES), F32).at[:, :2 * DN_HEADS].set(dn_w_in[0, :, 4 * dn_dim:]).astype(BF16)
    q, k, v, z, ba = _gdn_proj(h, row(mix_norm_g[0]), w_main, w_ba, dn_conv_w[0].astype(F32), seq)
    o = _gdn_core(q, k, v, z, ba, _pad_lanes(dn_a_log[0], DN_HEADS), _pad_lanes(dn_dt_bias[0], DN_HEADS),
                  row(dn_out_norm_g[0]), batch, seq)
    h = _out_mlp(o, h, dn_w_out[0].astype(BF16), row(mlp_norm_g[0]), w_up, w_down, 0,
                 row(final_norm_g), final_norm=False)

    (qkv,) = _rms_proj(h, row(mix_norm_g[1]), [mb_w_in[0].astype(BF16)], [BF16])
    o = _moba_attn(qkv, batch, seq)
    h = _out_mlp(o, h, mb_w_out[0].astype(BF16), row(mlp_norm_g[1]), w_up, w_down, 1,
                 row(final_norm_g), final_norm=True)
    return h.reshape(batch, seq, d)
```

```python
import functools
import math

import jax
import jax.numpy as jnp
from jax import lax
from jax.experimental import pallas as pl
from jax.experimental.pallas import tpu as pltpu

F32 = jnp.float32
BF16 = jnp.bfloat16

RMS_EPS = 1e-6
L2_EPS = 1e-6

DN_HEADS = 8
DN_HEAD_DIM = 128
DN_CONV = 4
GDN_CHUNK = 128
MB_HEADS = 8
MB_BLOCK = 256
MB_TOPK = 3

LANES = 128
SUBLANES = 8
VMEM_LIMIT_BYTES = 56 * 1024 * 1024

ROW_TILE = 1024
GDN_PROJ_ROW_TILE = 512
COL_TILE = 512
GDN_PROJ_COL_TILE = 256
GDN_TIME_TILE = 512
GDN_GROUP_CHUNKS = 2
MOBA_SUM_ROWS = 16
MOBA_SCORE_LEAD = 4


def _dot(a, b):
    return jnp.dot(a, b, preferred_element_type=F32)


def _dot_nt(a, b):
    return lax.dot_general(a, b, (((1,), (1,)), ((), ())), preferred_element_type=F32)


def _dot_tn(a, b):
    return lax.dot_general(a, b, (((0,), (0,)), ((), ())), preferred_element_type=F32)


def _rms_scale(x):
    return x * lax.rsqrt(jnp.mean(x * x, axis=-1, keepdims=True) + RMS_EPS)


def _const_spec(shape):
    return pl.BlockSpec(shape, lambda *_: (0,) * len(shape), pipeline_mode=pl.Buffered(1))


def _layer_spec(stacked, layer):
    return pl.BlockSpec((None,) + stacked.shape[1:], lambda *_: (layer, 0, 0), pipeline_mode=pl.Buffered(1))


def _rms_proj_kernel(x_ref, g_ref, *refs, n_w):
    w_refs, o_refs = refs[:n_w], refs[n_w:]
    hn = (_rms_scale(x_ref[...]) * g_ref[...]).astype(BF16)
    for w_ref, o_ref in zip(w_refs, o_refs):
        n_cols = w_ref.shape[1]
        tn = min(COL_TILE, n_cols)
        for c in range(n_cols // tn):
            cols = slice(c * tn, (c + 1) * tn)
            o_ref[:, cols] = _dot(hn, w_ref[:, cols]).astype(o_ref.dtype)


def _rms_proj(x, g, ws, out_dtypes):
    n, d = x.shape
    tm = min(ROW_TILE, n)
    in_specs = [pl.BlockSpec((tm, d), lambda i: (i, 0)), _const_spec((1, d))]
    in_specs += [_const_spec(w.shape) for w in ws]
    out_specs = [pl.BlockSpec((tm, w.shape[1]), lambda i: (i, 0)) for w in ws]
    out_shape = [jax.ShapeDtypeStruct((n, w.shape[1]), dt) for w, dt in zip(ws, out_dtypes)]
    return pl.pallas_call(
        functools.partial(_rms_proj_kernel, n_w=len(ws)),
        grid=(n // tm,),
        in_specs=in_specs,
        out_specs=out_specs,
        out_shape=out_shape,
        compiler_params=pltpu.CompilerParams(
            dimension_semantics=("parallel",), vmem_limit_bytes=VMEM_LIMIT_BYTES),
        name="rms_proj",
    )(x, g, *ws)


def _gdn_proj_kernel(x_ref, g_ref, w_ref, wba_ref, cw_ref, q_ref, k_ref, v_ref, z_ref, ba_ref, halo_ref,
                     *, n_heads, tiles_per_seq):
    tm = x_ref.shape[0]
    width = q_ref.shape[1]
    dh = width // n_heads
    tc = min(GDN_PROJ_COL_TILE, width)
    groups = tm // SUBLANES
    hn = (_rms_scale(x_ref[...]) * g_ref[...]).astype(BF16)

    @pl.when(lax.rem(pl.program_id(0), tiles_per_seq) == 0)
    def _():
        halo_ref[...] = jnp.zeros_like(halo_ref)

    sub = lax.broadcasted_iota(jnp.int32, (groups, SUBLANES, tc), 1)
    for sec, o_ref in enumerate((q_ref, k_ref, v_ref)):
        for c in range(width // tc):
            cols = slice(sec * width + c * tc, sec * width + (c + 1) * tc)
            y = _dot(hn, w_ref[:, cols])
            yg = jnp.concatenate([y, halo_ref[:, cols]], axis=0).reshape(groups + 1, SUBLANES, tc)
            halo_ref[:, cols] = y[tm - SUBLANES:, :]
            acc = y * cw_ref[DN_CONV - 1:DN_CONV, cols]
            for s in range(1, DN_CONV):
                r = pltpu.roll(yg, s, axis=1)
                prev = jnp.concatenate([r[groups:], r[:groups - 1]], axis=0)
                shifted = jnp.where(sub >= s, r[:groups], prev).reshape(tm, tc)
                acc = acc + shifted * cw_ref[DN_CONV - 1 - s:DN_CONV - s, cols]
            a = acc * jax.nn.sigmoid(acc)
            if sec < 2:
                parts = []
                for hh in range(tc // dh):
                    p = a[:, hh * dh:(hh + 1) * dh]
                    inv = lax.rsqrt(jnp.sum(p * p, axis=-1, keepdims=True) + L2_EPS)
                    parts.append(p * (inv * dh ** -0.5 if sec == 0 else inv))
                a = jnp.concatenate(parts, axis=1)
            o_ref[:, c * tc:(c + 1) * tc] = a.astype(o_ref.dtype)
    for c in range(width // tc):
        z_ref[:, c * tc:(c + 1) * tc] = _dot(hn, w_ref[:, 3 * width + c * tc:3 * width + (c + 1) * tc])
    ba_ref[...] = _dot(hn, wba_ref[...])


def _gdn_proj(x, g, w_main, w_ba, conv_w, seq):
    n, d = x.shape
    width = w_main.shape[1] // 4
    tm = min(GDN_PROJ_ROW_TILE, seq)
    row = lambda i: (i, 0)
    return pl.pallas_call(
        functools.partial(_gdn_proj_kernel, n_heads=DN_HEADS, tiles_per_seq=seq // tm),
        grid=(n // tm,),
        in_specs=[pl.BlockSpec((tm, d), row), _const_spec((1, d)), _const_spec(w_main.shape),
                  _const_spec(w_ba.shape), _const_spec(conv_w.shape)],
        out_specs=[pl.BlockSpec((tm, width), row)] * 4 + [pl.BlockSpec((tm, LANES), row)],
        out_shape=[jax.ShapeDtypeStruct((n, width), BF16)] * 3
                  + [jax.ShapeDtypeStruct((n, width), F32), jax.ShapeDtypeStruct((n, LANES), F32)],
        scratch_shapes=[pltpu.VMEM((SUBLANES, 3 * width), F32)],
        compiler_params=pltpu.CompilerParams(
            dimension_semantics=("arbitrary",), vmem_limit_bytes=VMEM_LIMIT_BYTES),
        name="gdn_proj",
    )(x, g, w_main, w_ba, conv_w)


def _out_mlp_kernel(o_ref, h_ref, wout_ref, g_ref, wup_ref, wdn_ref, gf_ref, out_ref, *, final_norm):
    h1 = h_ref[...] + _dot(o_ref[...], wout_ref[...])
    hn = (_rms_scale(h1) * g_ref[...]).astype(BF16)
    out_ref[...] = h1
    d_ff = wup_ref.shape[1]
    tf = min(COL_TILE, d_ff)
    for c in range(d_ff // tf):
        cols = slice(c * tf, (c + 1) * tf)
        act = jnp.square(jnp.maximum(_dot(hn, wup_ref[:, cols]), 0.0)).astype(BF16)
        out_ref[...] += _dot(act, wdn_ref[cols, :])
    if final_norm:
        out_ref[...] = _rms_scale(out_ref[...]) * gf_ref[...]


def _out_mlp(o, h, w_out, g, w_up, w_down, layer, g_final, final_norm):
    n, d = h.shape
    tm = min(ROW_TILE, n)
    row = lambda i: (i, 0)
    return pl.pallas_call(
        functools.partial(_out_mlp_kernel, final_norm=final_norm),
        grid=(n // tm,),
        in_specs=[pl.BlockSpec((tm, o.shape[1]), row), pl.BlockSpec((tm, d), row),
                  _const_spec(w_out.shape), _const_spec((1, d)),
                  _layer_spec(w_up, layer), _layer_spec(w_down, layer), _const_spec((1, d))],
        out_specs=pl.BlockSpec((tm, d), row),
        out_shape=jax.ShapeDtypeStruct((n, d), F32),
        compiler_params=pltpu.CompilerParams(
            dimension_semantics=("parallel",), vmem_limit_bytes=VMEM_LIMIT_BYTES),
        name="out_mlp",
    )(o, h, w_out, g, w_up, w_down, g_final)


def _unit_lower_inverse_all(a_lows, row, col, between=()):
    c = a_lows[0].shape[0]
    eye = jnp.where(row == col, 1.0, 0.0).astype(F32)
    negs = [(-a).astype(BF16) for a in a_lows]
    between = list(between)
    ms = None
    s = 1
    while s < c:
        same_pair = (row // (2 * s)) == (col // (2 * s))
        off = same_pair & ((row % (2 * s)) >= s) & ((col % (2 * s)) < s)
        if ms is None:
            ms = [(eye - jnp.where(off, a, 0.0)).astype(BF16) for a in a_lows]
        else:
            ys = [_dot(m, n).astype(BF16) for m, n in zip(ms, negs)]
            ms = [jnp.where(off, _dot(y, m).astype(BF16), m) for y, m in zip(ys, ms)]
            if between:
                between.pop(0)()
        s *= 2
    for work in between:
        work()
    return ms


def _gdn_kernel(q_ref, k_ref, v_ref, z_ref, ba_ref, alog_ref, dtb_ref, ong_ref,
                o_ref, state_ref, *, n_heads, chunk):
    t_idx = pl.program_id(1)
    tt = q_ref.shape[0]
    dh = q_ref.shape[1] // n_heads
    n_chunks = tt // chunk

    @pl.when(t_idx == 0)
    def _():
        state_ref[...] = jnp.zeros_like(state_ref)

    ba = ba_ref[...]
    lane = lax.broadcasted_iota(jnp.int32, ba.shape, 1)
    trow = lax.broadcasted_iota(jnp.int32, ba.shape, 0)
    beta_all = jax.nn.sigmoid(ba)
    sp_in = ba + dtb_ref[...]
    softplus = jnp.maximum(sp_in, 0.0) + jnp.log(1.0 + jnp.exp(-jnp.abs(sp_in)))
    g_all = jnp.where((lane >= n_heads) & (lane < 2 * n_heads), -jnp.exp(alog_ref[...]) * softplus, 0.0)
    gc_all = g_all
    s = 1
    while s < chunk:
        shifted = pltpu.roll(gc_all, s, axis=0)
        gc_all = gc_all + jnp.where((trow % chunk) >= s, shifted, 0.0)
        s *= 2
    gc_all_t = gc_all.T
    beta_all_t = beta_all.T
    g_last_all = jnp.concatenate(
        [jnp.broadcast_to(gc_all[(c + 1) * chunk - 1:(c + 1) * chunk, :], (chunk, LANES)) for c in range(n_chunks)],
        axis=0)
    e_gc_all = jnp.exp(gc_all)
    e_rem_all = jnp.exp(g_last_all - gc_all)
    e_last_all = jnp.exp(g_last_all)

    row = lax.broadcasted_iota(jnp.int32, (chunk, chunk), 0)
    col = lax.broadcasted_iota(jnp.int32, (chunk, chunk), 1)

    rows_of = lambda c: slice(c * chunk, (c + 1) * chunk)
    lanes_of = lambda h: slice(h * dh, (h + 1) * dh)
    heads = range(n_heads)

    def recurrence_steps(c, k16, q16, attn16, sol):
        rows = rows_of(c)
        st = {}

        def read_state():
            st["states"] = [state_ref[h] for h in heads]
            st["s16"] = [s.astype(BF16) for s in st["states"]]
            st["wq"] = [_dot(jnp.concatenate([sol[h][:, dh:].astype(BF16), q16[h]], axis=0), st["s16"][h])
                        for h in heads]

        def new_values():
            st["v_new"] = [sol[h][:, :dh] - st["wq"][h][:chunk] for h in heads]
            vn16 = [v.astype(BF16) for v in st["v_new"]]
            st["o_inter"] = [wq[chunk:] for wq in st["wq"]]
            kd_t = [(k16[h].astype(F32) * e_rem_all[rows, n_heads + h:n_heads + h + 1]).T.astype(BF16)
                    for h in heads]
            both = [_dot(jnp.concatenate([attn16[h], kd_t[h]], axis=0), vn16[h]) for h in heads]
            st["o_intra"] = [b[:chunk] for b in both]
            st["s_upd"] = [b[chunk:] for b in both]

        def write_out():
            for h in heads:
                gl = n_heads + h
                state_ref[h] = st["states"][h] * e_last_all[c * chunk:c * chunk + 1, gl:gl + 1] + st["s_upd"][h]
                z = z_ref[rows, lanes_of(h)]
                o = st["o_inter"][h] * e_gc_all[rows, gl:gl + 1] + st["o_intra"][h]
                o = _rms_scale(o) * ong_ref[...] * (z * jax.nn.sigmoid(z))
                o_ref[rows, lanes_of(h)] = o.astype(o_ref.dtype)

        return [read_state, new_values, write_out]

    group_chunks = [list(range(c0, min(c0 + GDN_GROUP_CHUNKS, n_chunks)))
                    for c0 in range(0, n_chunks, GDN_GROUP_CHUNKS)]
    group_pairs = [[(c, h) for c in cs for h in heads] for cs in group_chunks]
    k16 = [[k_ref[rows_of(c), lanes_of(h)] for c, h in pairs] for pairs in group_pairs]
    q16 = [[q_ref[rows_of(c), lanes_of(h)] for c, h in pairs] for pairs in group_pairs]
    kq = [[_dot_nt(jnp.concatenate([k, q], axis=0), k) for q, k in zip(qs, ks)] for qs, ks in zip(q16, k16)]
    pending = []
    for g, pairs in enumerate(group_pairs):
        a_low, attn16 = [], []
        for (c, h), kq_p in zip(pairs, kq[g]):
            kk_p, qk_p = kq_p[:chunk], kq_p[chunk:]
            gl = n_heads + h
            gc = gc_all[rows_of(c), gl:gl + 1]
            gc_row = gc_all_t[gl:gl + 1, rows_of(c)]
            decay = jnp.exp(jnp.where(row >= col, gc - gc_row, -jnp.inf))
            a_low.append(jnp.where(row > col, kk_p * (decay * beta_all[rows_of(c), h:h + 1]), 0.0))
            attn16.append((qk_p * decay).astype(BF16))
        t_inv = _unit_lower_inverse_all(a_low, row, col, between=pending)
        sol = []
        for (c, h), t, k in zip(pairs, t_inv, k16[g]):
            gl = n_heads + h
            t_beta = (t.astype(F32) * beta_all_t[h:h + 1, rows_of(c)]).astype(BF16)
            k_dec = (k.astype(F32) * e_gc_all[rows_of(c), gl:gl + 1]).astype(BF16)
            sol.append(_dot(t_beta, jnp.concatenate([v_ref[rows_of(c), lanes_of(h)], k_dec], axis=-1)))
        pending = []
        for ci, c in enumerate(group_chunks[g]):
            part = slice(ci * n_heads, (ci + 1) * n_heads)
            pending += recurrence_steps(c, k16[g][part], q16[g][part], attn16[part], sol[part])
    for work in pending:
        work()


def _gdn_core(q, k, v, z, ba, a_log_row, dt_bias_row, out_norm_g, batch, seq):
    n_heads, dh = DN_HEADS, DN_HEAD_DIM
    width = n_heads * dh
    tt = min(GDN_TIME_TILE, seq)
    nt = seq // tt
    tile = lambda b, t: (b * nt + t, 0)
    return pl.pallas_call(
        functools.partial(_gdn_kernel, n_heads=n_heads, chunk=min(GDN_CHUNK, tt)),
        grid=(batch, nt),
        in_specs=[pl.BlockSpec((tt, width), tile)] * 4 + [pl.BlockSpec((tt, LANES), tile),
                  _const_spec((1, LANES)), _const_spec((1, LANES)), _const_spec((1, dh))],
        out_specs=pl.BlockSpec((tt, width), tile),
        out_shape=jax.ShapeDtypeStruct((batch * seq, width), BF16),
        scratch_shapes=[pltpu.VMEM((n_heads, dh, dh), F32)],
        compiler_params=pltpu.CompilerParams(
            dimension_semantics=("parallel", "arbitrary"), vmem_limit_bytes=VMEM_LIMIT_BYTES),
        name="gdn_core",
    )(q, k, v, z, ba, a_log_row, dt_bias_row, out_norm_g)


def _split3_f32(x):
    hi = x.astype(BF16).astype(F32)
    mid = (x - hi).astype(BF16).astype(F32)
    return hi, mid, ((x - hi) - mid).astype(BF16).astype(F32)


def _moba_kernel(q_ref, k_ref, v_ref, o_ref, kmean_ref, vt_ref, maskb_ref, kaug_ref, qaug_ref,
                 m_ref, acc_ref, *, n_heads, top_k):
    i = pl.program_id(1)
    blk = q_ref.shape[0]
    seq = k_ref.shape[0]
    dh = k_ref.shape[1] // n_heads
    nb = seq // blk
    scale2 = dh ** -0.5 * math.log2(math.e)
    heads = range(n_heads)
    lanes = [slice(h * dh, (h + 1) * dh) for h in heads]
    slope2 = [2.0 ** (-8.0 * (h + 1) / n_heads) * math.log2(math.e) for h in heads]

    krow = lax.broadcasted_iota(jnp.int32, (blk, blk), 0)
    qcol = lax.broadcasted_iota(jnp.int32, (blk, blk), 1)
    aug_lane = lax.broadcasted_iota(jnp.int32, (blk, dh), 1)

    @pl.when(i == 0)
    def _():
        r = lax.broadcasted_iota(jnp.int32, (nb, seq), 0)
        c = lax.broadcasted_iota(jnp.int32, (nb, seq), 1)
        ind = jnp.where(c // blk == r, 1.0, 0.0).astype(BF16)
        key_pos = lax.broadcasted_iota(jnp.int32, (blk, dh), 0).astype(F32)
        for h in heads:
            kmean_ref[h] = (_dot(ind, k_ref[:, lanes[h]]) * (1.0 / blk)).astype(BF16)
            hi, mid, lo = _split3_f32(slope2[h] * key_pos)
            kaug_ref[h] = jnp.where(aug_lane == 0, hi, jnp.where(aug_lane == 1, mid,
                                                                 jnp.where(aug_lane == 2, lo, 0.0))).astype(BF16)
            for j in range(nb):
                vt = v_ref[j * blk:(j + 1) * blk, lanes[h]].astype(F32).T.astype(BF16)
                vt_ref[h, j] = jnp.concatenate([vt, jnp.ones((MOBA_SUM_ROWS, blk), BF16)], axis=0)

    qs = [q_ref[:, lanes[h]] for h in heads]
    ones_cols = jnp.where(aug_lane < 3, 1.0, 0.0).astype(BF16)
    for h in heads:
        qaug_ref[h] = jnp.concatenate([(qs[h].astype(F32) * scale2).astype(BF16), ones_cols], axis=1)

    gates = [_dot_nt(kmean_ref[h], qs[h]) for h in heads]
    bidx = lax.broadcasted_iota(jnp.int32, (nb, blk), 0)
    past = bidx < i
    for h in heads:
        gate = jnp.where(past, gates[h], -jnp.inf)
        rank = jnp.zeros(gate.shape, F32)
        for jp in range(nb):
            other = gate[jp:jp + 1, :]
            beats = (other > gate) | ((other == gate) & (jp < bidx))
            rank = rank + jnp.where(beats, 1.0, 0.0)
        maskb_ref[h] = jnp.where(past & (rank < top_k), 0.0, -jnp.inf)

    def pipelined(tiles, score_mm, rest):
        raw = {}
        for step in range(len(tiles) + MOBA_SCORE_LEAD):
            if step < len(tiles):
                raw[step] = score_mm(*tiles[step])
            if step >= MOBA_SCORE_LEAD:
                rest(*tiles[step - MOBA_SCORE_LEAD], raw.pop(step - MOBA_SCORE_LEAD))

    def scores(h, j):
        rows = pl.ds(pl.multiple_of(j * blk, blk), blk)
        k_aug = jnp.concatenate([k_ref[rows, lanes[h]], kaug_ref[h]], axis=1)
        return _dot_nt(k_aug, qaug_ref[h])

    def own_rest(h, raw):
        s = jnp.where(krow <= qcol, raw, -jnp.inf)
        m = jnp.max(s, axis=0, keepdims=True)
        m_ref[h] = m
        acc_ref[h] = _dot(vt_ref[h, i], jnp.exp2(s - m).astype(BF16))

    def past_rest(h, j, raw):
        row_bias = maskb_ref[h, pl.ds(j, 1), :] + slope2[h] * ((j - i) * blk).astype(F32)
        m_old = m_ref[h]
        m_new = jnp.maximum(m_old, jnp.max(raw, axis=0, keepdims=True) + row_bias)
        m_ref[h] = m_new
        p = jnp.exp2(raw + (row_bias - m_new)).astype(BF16)
        acc_ref[h] = acc_ref[h] * jnp.exp2(m_old - m_new) + _dot(vt_ref[h, j], p)

    def blocks(js, with_own):
        tiles = ([(h, None) for h in heads] if with_own else []) + [(h, j) for j in js for h in heads]
        pipelined(tiles, lambda h, j: scores(h, i if j is None else j),
                  lambda h, j, raw: own_rest(h, raw) if j is None else past_rest(h, j, raw))

    quads = i >> 2
    for rem in range(4):
        @pl.when((i & 3) == rem)
        def _():
            blocks([quads * 4 + d for d in range(rem)], with_own=True)

    def quad_body(jj, carry):
        blocks([4 * jj + d for d in range(4)], with_own=False)
        return carry

    lax.fori_loop(0, quads, quad_body, 0)

    for h in heads:
        acc = acc_ref[h]
        o_ref[:, lanes[h]] = (acc[:dh] / acc[dh:dh + 1]).T.astype(o_ref.dtype)


def _moba_attn(qkv, batch, seq):
    n_heads = MB_HEADS
    width = qkv.shape[1] // 3
    dh = width // n_heads
    blk = min(MB_BLOCK, seq)
    nb = seq // blk
    return pl.pallas_call(
        functools.partial(_moba_kernel, n_heads=n_heads, top_k=min(MB_TOPK, nb)),
        grid=(batch, nb),
        in_specs=[pl.BlockSpec((blk, width), lambda b, i: (b * nb + i, 0)),
                  pl.BlockSpec((seq, width), lambda b, i: (b, 1)),
                  pl.BlockSpec((seq, width), lambda b, i: (b, 2))],
        out_specs=pl.BlockSpec((blk, width), lambda b, i: (b * nb + i, 0)),
        out_shape=jax.ShapeDtypeStruct((batch * seq, width), BF16),
        scratch_shapes=[pltpu.VMEM((n_heads, nb, dh), BF16),
                        pltpu.VMEM((n_heads, nb, dh + MOBA_SUM_ROWS, blk), BF16),
                        pltpu.VMEM((n_heads, nb, blk), F32),
                        pltpu.VMEM((n_heads, blk, dh), BF16),
                        pltpu.VMEM((n_heads, blk, 2 * dh), BF16),
                        pltpu.VMEM((n_heads, 1, blk), F32),
                        pltpu.VMEM((n_heads, dh + MOBA_SUM_ROWS, blk), F32)],
        compiler_params=pltpu.CompilerParams(
            dimension_semantics=("parallel", "arbitrary"), vmem_limit_bytes=VMEM_LIMIT_BYTES),
        name="moba_attn",
    )(qkv, qkv, qkv)


def _pad_lanes(row_vec, offset):
    return jnp.zeros((1, LANES), F32).at[0, offset:offset + row_vec.shape[0]].set(row_vec.astype(F32))


def kernel(x, mix_norm_g, mlp_norm_g, dn_w_in, dn_conv_w, dn_a_log, dn_dt_bias, dn_out_norm_g, dn_w_out,
           mb_w_in, mb_w_out, mlp_w_up, mlp_w_down, final_norm_g):
    batch, seq, d = x.shape
    n = batch * seq
    h = x.reshape(n, d)
    dn_dim = DN_HEADS * DN_HEAD_DIM
    row = lambda v: v.reshape(1, -1).astype(F32)
    w_up, w_down = mlp_w_up.astype(BF16), mlp_w_down.astype(BF16)

    w_main = dn_w_in[0, :, :4 * dn_dim].astype(BF16)
    w_ba = jnp.zeros((d, LANES), F32).at[:, :2 * DN_HEADS].set(dn_w_in[0, :, 4 * dn_dim:]).astype(BF16)
    q, k, v, z, ba = _gdn_proj(h, row(mix_norm_g[0]), w_main, w_ba, dn_conv_w[0].astype(F32), seq)
    o = _gdn_core(q, k, v, z, ba, _pad_lanes(dn_a_log[0], DN_HEADS), _pad_lanes(dn_dt_bias[0], DN_HEADS),
                  row(dn_out_norm_g[0]), batch, seq)
    h = _out_mlp(o, h, dn_w_out[0].astype(BF16), row(mlp_norm_g[0]), w_up, w_down, 0,
                 row(final_norm_g), final_norm=False)

    (qkv,) = _rms_proj(h, row(mix_norm_g[1]), [mb_w_in[0].astype(BF16)], [BF16])
    o = _moba_attn(qkv, batch, seq)
    h = _out_mlp(o, h, mb_w_out[0].astype(BF16), row(mlp_norm_g[1]), w_up, w_down, 1,
                 row(final_norm_g), final_norm=True)
    return h.reshape(batch, seq, d)
```

```python
import functools
import math

import jax
import jax.numpy as jnp
from jax import lax
from jax.experimental import pallas as pl
from jax.experimental.pallas import tpu as pltpu

F32 = jnp.float32
BF16 = jnp.bfloat16

RMS_EPS = 1e-6
L2_EPS = 1e-6

DN_HEADS = 8
DN_HEAD_DIM = 128
DN_CONV = 4
GDN_CHUNK = 128
MB_HEADS = 8
MB_BLOCK = 256
MB_TOPK = 3

LANES = 128
SUBLANES = 8
VMEM_LIMIT_BYTES = 56 * 1024 * 1024

ROW_TILE = 1024
GDN_PROJ_ROW_TILE = 512
COL_TILE = 512
GDN_PROJ_COL_TILE = 256
GDN_TIME_TILE = 512
GDN_GROUP_CHUNKS = 2
MOBA_SUM_ROWS = 16
MOBA_HEAD_TRIP_LEAD = 6
MOBA_LOOP_LEAD = 4


def _dot(a, b):
    return jnp.dot(a, b, preferred_element_type=F32)


def _dot_nt(a, b):
    return lax.dot_general(a, b, (((1,), (1,)), ((), ())), preferred_element_type=F32)


def _dot_tn(a, b):
    return lax.dot_general(a, b, (((0,), (0,)), ((), ())), preferred_element_type=F32)


def _rms_scale(x):
    return x * lax.rsqrt(jnp.mean(x * x, axis=-1, keepdims=True) + RMS_EPS)


def _const_spec(shape):
    return pl.BlockSpec(shape, lambda *_: (0,) * len(shape), pipeline_mode=pl.Buffered(1))


def _layer_spec(stacked, layer):
    return pl.BlockSpec((None,) + stacked.shape[1:], lambda *_: (layer, 0, 0), pipeline_mode=pl.Buffered(1))


def _rms_proj_kernel(x_ref, g_ref, *refs, n_w):
    w_refs, o_refs = refs[:n_w], refs[n_w:]
    hn = (_rms_scale(x_ref[...]) * g_ref[...]).astype(BF16)
    for w_ref, o_ref in zip(w_refs, o_refs):
        n_cols = w_ref.shape[1]
        tn = min(COL_TILE, n_cols)
        for c in range(n_cols // tn):
            cols = slice(c * tn, (c + 1) * tn)
            o_ref[:, cols] = _dot(hn, w_ref[:, cols]).astype(o_ref.dtype)


def _rms_proj(x, g, ws, out_dtypes):
    n, d = x.shape
    tm = min(ROW_TILE, n)
    in_specs = [pl.BlockSpec((tm, d), lambda i: (i, 0)), _const_spec((1, d))]
    in_specs += [_const_spec(w.shape) for w in ws]
    out_specs = [pl.BlockSpec((tm, w.shape[1]), lambda i: (i, 0)) for w in ws]
    out_shape = [jax.ShapeDtypeStruct((n, w.shape[1]), dt) for w, dt in zip(ws, out_dtypes)]
    return pl.pallas_call(
        functools.partial(_rms_proj_kernel, n_w=len(ws)),
        grid=(n // tm,),
        in_specs=in_specs,
        out_specs=out_specs,
        out_shape=out_shape,
        compiler_params=pltpu.CompilerParams(
            dimension_semantics=("parallel",), vmem_limit_bytes=VMEM_LIMIT_BYTES),
        name="rms_proj",
    )(x, g, *ws)


def _gdn_proj_kernel(x_ref, g_ref, w_ref, wba_ref, cw_ref, q_ref, k_ref, v_ref, z_ref, ba_ref, halo_ref,
                     *, n_heads, tiles_per_seq):
    tm = x_ref.shape[0]
    width = q_ref.shape[1]
    dh = width // n_heads
    tc = min(GDN_PROJ_COL_TILE, width)
    groups = tm // SUBLANES
    hn = (_rms_scale(x_ref[...]) * g_ref[...]).astype(BF16)

    @pl.when(lax.rem(pl.program_id(0), tiles_per_seq) == 0)
    def _():
        halo_ref[...] = jnp.zeros_like(halo_ref)

    sub = lax.broadcasted_iota(jnp.int32, (groups, SUBLANES, tc), 1)
    for sec, o_ref in enumerate((q_ref, k_ref, v_ref)):
        for c in range(width // tc):
            cols = slice(sec * width + c * tc, sec * width + (c + 1) * tc)
            y = _dot(hn, w_ref[:, cols])
            yg = jnp.concatenate([y, halo_ref[:, cols]], axis=0).reshape(groups + 1, SUBLANES, tc)
            halo_ref[:, cols] = y[tm - SUBLANES:, :]
            acc = y * cw_ref[DN_CONV - 1:DN_CONV, cols]
            for s in range(1, DN_CONV):
                r = pltpu.roll(yg, s, axis=1)
                prev = jnp.concatenate([r[groups:], r[:groups - 1]], axis=0)
                shifted = jnp.where(sub >= s, r[:groups], prev).reshape(tm, tc)
                acc = acc + shifted * cw_ref[DN_CONV - 1 - s:DN_CONV - s, cols]
            a = acc * jax.nn.sigmoid(acc)
            if sec < 2:
                parts = []
                for hh in range(tc // dh):
                    p = a[:, hh * dh:(hh + 1) * dh]
                    inv = lax.rsqrt(jnp.sum(p * p, axis=-1, keepdims=True) + L2_EPS)
                    parts.append(p * (inv * dh ** -0.5 if sec == 0 else inv))
                a = jnp.concatenate(parts, axis=1)
            o_ref[:, c * tc:(c + 1) * tc] = a.astype(o_ref.dtype)
    for c in range(width // tc):
        z_ref[:, c * tc:(c + 1) * tc] = _dot(hn, w_ref[:, 3 * width + c * tc:3 * width + (c + 1) * tc])
    ba_ref[...] = _dot(hn, wba_ref[...])


def _gdn_proj(x, g, w_main, w_ba, conv_w, seq):
    n, d = x.shape
    width = w_main.shape[1] // 4
    tm = min(GDN_PROJ_ROW_TILE, seq)
    row = lambda i: (i, 0)
    return pl.pallas_call(
        functools.partial(_gdn_proj_kernel, n_heads=DN_HEADS, tiles_per_seq=seq // tm),
        grid=(n // tm,),
        in_specs=[pl.BlockSpec((tm, d), row), _const_spec((1, d)), _const_spec(w_main.shape),
                  _const_spec(w_ba.shape), _const_spec(conv_w.shape)],
        out_specs=[pl.BlockSpec((tm, width), row)] * 4 + [pl.BlockSpec((tm, LANES), row)],
        out_shape=[jax.ShapeDtypeStruct((n, width), BF16)] * 3
                  + [jax.ShapeDtypeStruct((n, width), F32), jax.ShapeDtypeStruct((n, LANES), F32)],
        scratch_shapes=[pltpu.VMEM((SUBLANES, 3 * width), F32)],
        compiler_params=pltpu.CompilerParams(
            dimension_semantics=("arbitrary",), vmem_limit_bytes=VMEM_LIMIT_BYTES),
        name="gdn_proj",
    )(x, g, w_main, w_ba, conv_w)


def _out_mlp_kernel(o_ref, h_ref, wout_ref, g_ref, wup_ref, wdn_ref, gf_ref, out_ref, *, final_norm):
    h1 = h_ref[...] + _dot(o_ref[...], wout_ref[...])
    hn = (_rms_scale(h1) * g_ref[...]).astype(BF16)
    out_ref[...] = h1
    d_ff = wup_ref.shape[1]
    tf = min(COL_TILE, d_ff)
    for c in range(d_ff // tf):
        cols = slice(c * tf, (c + 1) * tf)
        act = jnp.square(jnp.maximum(_dot(hn, wup_ref[:, cols]), 0.0)).astype(BF16)
        out_ref[...] += _dot(act, wdn_ref[cols, :])
    if final_norm:
        out_ref[...] = _rms_scale(out_ref[...]) * gf_ref[...]


def _out_mlp(o, h, w_out, g, w_up, w_down, layer, g_final, final_norm):
    n, d = h.shape
    tm = min(ROW_TILE, n)
    row = lambda i: (i, 0)
    return pl.pallas_call(
        functools.partial(_out_mlp_kernel, final_norm=final_norm),
        grid=(n // tm,),
        in_specs=[pl.BlockSpec((tm, o.shape[1]), row), pl.BlockSpec((tm, d), row),
                  _const_spec(w_out.shape), _const_spec((1, d)),
                  _layer_spec(w_up, layer), _layer_spec(w_down, layer), _const_spec((1, d))],
        out_specs=pl.BlockSpec((tm, d), row),
        out_shape=jax.ShapeDtypeStruct((n, d), F32),
        compiler_params=pltpu.CompilerParams(
            dimension_semantics=("parallel",), vmem_limit_bytes=VMEM_LIMIT_BYTES),
        name="out_mlp",
    )(o, h, w_out, g, w_up, w_down, g_final)


def _unit_lower_inverse_all(a_lows, row, col, between=()):
    c = a_lows[0].shape[0]
    eye = jnp.where(row == col, 1.0, 0.0).astype(F32)
    negs = [(-a).astype(BF16) for a in a_lows]
    between = list(between)
    ms = None
    s = 1
    while s < c:
        same_pair = (row // (2 * s)) == (col // (2 * s))
        off = same_pair & ((row % (2 * s)) >= s) & ((col % (2 * s)) < s)
        if ms is None:
            ms = [(eye - jnp.where(off, a, 0.0)).astype(BF16) for a in a_lows]
        else:
            ys = [_dot(m, n).astype(BF16) for m, n in zip(ms, negs)]
            ms = [jnp.where(off, _dot(y, m).astype(BF16), m) for y, m in zip(ys, ms)]
            if between:
                between.pop(0)()
        s *= 2
    for work in between:
        work()
    return ms


def _gdn_kernel(q_ref, k_ref, v_ref, z_ref, ba_ref, alog_ref, dtb_ref, ong_ref,
                o_ref, state_ref, *, n_heads, chunk):
    t_idx = pl.program_id(1)
    tt = q_ref.shape[0]
    dh = q_ref.shape[1] // n_heads
    n_chunks = tt // chunk

    @pl.when(t_idx == 0)
    def _():
        state_ref[...] = jnp.zeros_like(state_ref)

    ba = ba_ref[...]
    lane = lax.broadcasted_iota(jnp.int32, ba.shape, 1)
    trow = lax.broadcasted_iota(jnp.int32, ba.shape, 0)
    beta_all = jax.nn.sigmoid(ba)
    sp_in = ba + dtb_ref[...]
    softplus = jnp.maximum(sp_in, 0.0) + jnp.log(1.0 + jnp.exp(-jnp.abs(sp_in)))
    g_all = jnp.where((lane >= n_heads) & (lane < 2 * n_heads), -jnp.exp(alog_ref[...]) * softplus, 0.0)
    gc_all = g_all
    s = 1
    while s < chunk:
        shifted = pltpu.roll(gc_all, s, axis=0)
        gc_all = gc_all + jnp.where((trow % chunk) >= s, shifted, 0.0)
        s *= 2
    gc_all_t = gc_all.T
    beta_all_t = beta_all.T
    g_last_all = jnp.concatenate(
        [jnp.broadcast_to(gc_all[(c + 1) * chunk - 1:(c + 1) * chunk, :], (chunk, LANES)) for c in range(n_chunks)],
        axis=0)
    e_gc_all = jnp.exp(gc_all)
    e_rem_all = jnp.exp(g_last_all - gc_all)
    e_last_all = jnp.exp(g_last_all)

    row = lax.broadcasted_iota(jnp.int32, (chunk, chunk), 0)
    col = lax.broadcasted_iota(jnp.int32, (chunk, chunk), 1)

    rows_of = lambda c: slice(c * chunk, (c + 1) * chunk)
    lanes_of = lambda h: slice(h * dh, (h + 1) * dh)
    heads = range(n_heads)

    def recurrence_steps(c, k16, q16, attn16, sol):
        rows = rows_of(c)
        st = {}

        def read_state():
            st["states"] = [state_ref[h] for h in heads]
            st["s16"] = [s.astype(BF16) for s in st["states"]]
            st["wq"] = [_dot(jnp.concatenate([sol[h][:, dh:].astype(BF16), q16[h]], axis=0), st["s16"][h])
                        for h in heads]

        def new_values():
            st["v_new"] = [sol[h][:, :dh] - st["wq"][h][:chunk] for h in heads]
            vn16 = [v.astype(BF16) for v in st["v_new"]]
            st["o_inter"] = [wq[chunk:] for wq in st["wq"]]
            kd_t = [(k16[h].astype(F32) * e_rem_all[rows, n_heads + h:n_heads + h + 1]).T.astype(BF16)
                    for h in heads]
            both = [_dot(jnp.concatenate([attn16[h], kd_t[h]], axis=0), vn16[h]) for h in heads]
            st["o_intra"] = [b[:chunk] for b in both]
            st["s_upd"] = [b[chunk:] for b in both]

        def write_out():
            for h in heads:
                gl = n_heads + h
                state_ref[h] = st["states"][h] * e_last_all[c * chunk:c * chunk + 1, gl:gl + 1] + st["s_upd"][h]
                z = z_ref[rows, lanes_of(h)]
                o = st["o_inter"][h] * e_gc_all[rows, gl:gl + 1] + st["o_intra"][h]
                o = _rms_scale(o) * ong_ref[...] * (z * jax.nn.sigmoid(z))
                o_ref[rows, lanes_of(h)] = o.astype(o_ref.dtype)

        return [read_state, new_values, write_out]

    group_chunks = [list(range(c0, min(c0 + GDN_GROUP_CHUNKS, n_chunks)))
                    for c0 in range(0, n_chunks, GDN_GROUP_CHUNKS)]
    group_pairs = [[(c, h) for c in cs for h in heads] for cs in group_chunks]
    k16 = [[k_ref[rows_of(c), lanes_of(h)] for c, h in pairs] for pairs in group_pairs]
    q16 = [[q_ref[rows_of(c), lanes_of(h)] for c, h in pairs] for pairs in group_pairs]
    kq = [[_dot_nt(jnp.concatenate([k, q], axis=0), k) for q, k in zip(qs, ks)] for qs, ks in zip(q16, k16)]
    pending = []
    for g, pairs in enumerate(group_pairs):
        a_low, attn16 = [], []
        for (c, h), kq_p in zip(pairs, kq[g]):
            kk_p, qk_p = kq_p[:chunk], kq_p[chunk:]
            gl = n_heads + h
            gc = gc_all[rows_of(c), gl:gl + 1]
            gc_row = gc_all_t[gl:gl + 1, rows_of(c)]
            decay = jnp.exp(jnp.where(row >= col, gc - gc_row, -jnp.inf))
            a_low.append(jnp.where(row > col, kk_p * (decay * beta_all[rows_of(c), h:h + 1]), 0.0))
            attn16.append((qk_p * decay).astype(BF16))
        t_inv = _unit_lower_inverse_all(a_low, row, col, between=pending)
        sol = []
        for (c, h), t, k in zip(pairs, t_inv, k16[g]):
            gl = n_heads + h
            t_beta = (t.astype(F32) * beta_all_t[h:h + 1, rows_of(c)]).astype(BF16)
            k_dec = (k.astype(F32) * e_gc_all[rows_of(c), gl:gl + 1]).astype(BF16)
            sol.append(_dot(t_beta, jnp.concatenate([v_ref[rows_of(c), lanes_of(h)], k_dec], axis=-1)))
        pending = []
        for ci, c in enumerate(group_chunks[g]):
            part = slice(ci * n_heads, (ci + 1) * n_heads)
            pending += recurrence_steps(c, k16[g][part], q16[g][part], attn16[part], sol[part])
    for work in pending:
        work()


def _gdn_core(q, k, v, z, ba, a_log_row, dt_bias_row, out_norm_g, batch, seq):
    n_heads, dh = DN_HEADS, DN_HEAD_DIM
    width = n_heads * dh
    tt = min(GDN_TIME_TILE, seq)
    nt = seq // tt
    tile = lambda b, t: (b * nt + t, 0)
    return pl.pallas_call(
        functools.partial(_gdn_kernel, n_heads=n_heads, chunk=min(GDN_CHUNK, tt)),
        grid=(batch, nt),
        in_specs=[pl.BlockSpec((tt, width), tile)] * 4 + [pl.BlockSpec((tt, LANES), tile),
                  _const_spec((1, LANES)), _const_spec((1, LANES)), _const_spec((1, dh))],
        out_specs=pl.BlockSpec((tt, width), tile),
        out_shape=jax.ShapeDtypeStruct((batch * seq, width), BF16),
        scratch_shapes=[pltpu.VMEM((n_heads, dh, dh), F32)],
        compiler_params=pltpu.CompilerParams(
            dimension_semantics=("parallel", "arbitrary"), vmem_limit_bytes=VMEM_LIMIT_BYTES),
        name="gdn_core",
    )(q, k, v, z, ba, a_log_row, dt_bias_row, out_norm_g)


def _split3_f32(x):
    hi = x.astype(BF16).astype(F32)
    mid = (x - hi).astype(BF16).astype(F32)
    return hi, mid, ((x - hi) - mid).astype(BF16).astype(F32)


def _moba_kernel(q_ref, k_ref, v_ref, o_ref, kmean_ref, vt_ref, maskb_ref, kaug_ref, qaug_ref,
                 m_ref, acc_ref, *, n_heads, top_k):
    i = pl.program_id(1)
    blk = q_ref.shape[0]
    seq = k_ref.shape[0]
    dh = k_ref.shape[1] // n_heads
    nb = seq // blk
    scale2 = dh ** -0.5 * math.log2(math.e)
    heads = range(n_heads)
    lanes = [slice(h * dh, (h + 1) * dh) for h in heads]
    slope2 = [2.0 ** (-8.0 * (h + 1) / n_heads) * math.log2(math.e) for h in heads]

    krow = lax.broadcasted_iota(jnp.int32, (blk, blk), 0)
    qcol = lax.broadcasted_iota(jnp.int32, (blk, blk), 1)
    aug_lane = lax.broadcasted_iota(jnp.int32, (blk, dh), 1)

    @pl.when(i == 0)
    def _():
        r = lax.broadcasted_iota(jnp.int32, (nb, seq), 0)
        c = lax.broadcasted_iota(jnp.int32, (nb, seq), 1)
        ind = jnp.where(c // blk == r, 1.0, 0.0).astype(BF16)
        key_pos = lax.broadcasted_iota(jnp.int32, (blk, dh), 0).astype(F32)
        for h in heads:
            kmean_ref[h] = (_dot(ind, k_ref[:, lanes[h]]) * (1.0 / blk)).astype(BF16)
            hi, mid, lo = _split3_f32(slope2[h] * key_pos)
            kaug_ref[h] = jnp.where(aug_lane == 0, hi, jnp.where(aug_lane == 1, mid,
                                                                 jnp.where(aug_lane == 2, lo, 0.0))).astype(BF16)
            for j in range(nb):
                vt = v_ref[j * blk:(j + 1) * blk, lanes[h]].astype(F32).T.astype(BF16)
                vt_ref[h, j] = jnp.concatenate([vt, jnp.ones((MOBA_SUM_ROWS, blk), BF16)], axis=0)

    qs = [q_ref[:, lanes[h]] for h in heads]
    ones_cols = jnp.where(aug_lane < 3, 1.0, 0.0).astype(BF16)
    for h in heads:
        qaug_ref[h] = jnp.concatenate([(qs[h].astype(F32) * scale2).astype(BF16), ones_cols], axis=1)

    gates = [_dot_nt(kmean_ref[h], qs[h]) for h in heads]
    bidx = lax.broadcasted_iota(jnp.int32, (nb, blk), 0)
    past = bidx < i
    for h in heads:
        gate = jnp.where(past, gates[h], -jnp.inf)
        rank = jnp.zeros(gate.shape, F32)
        for jp in range(nb):
            other = gate[jp:jp + 1, :]
            beats = (other > gate) | ((other == gate) & (jp < bidx))
            rank = rank + jnp.where(beats, 1.0, 0.0)
        maskb_ref[h] = jnp.where(past & (rank < top_k), 0.0, -jnp.inf)

    def pipelined(tiles, score_mm, rest, lead):
        raw = {}
        for step in range(len(tiles) + lead):
            if step < len(tiles):
                raw[step] = score_mm(*tiles[step])
            if step >= lead:
                rest(*tiles[step - lead], raw.pop(step - lead))

    def scores(h, j):
        rows = pl.ds(pl.multiple_of(j * blk, blk), blk)
        k_aug = jnp.concatenate([k_ref[rows, lanes[h]], kaug_ref[h]], axis=1)
        return _dot_nt(k_aug, qaug_ref[h])

    def own_rest(h, raw):
        s = jnp.where(krow <= qcol, raw, -jnp.inf)
        m = jnp.max(s, axis=0, keepdims=True)
        m_ref[h] = m
        acc_ref[h] = _dot(vt_ref[h, i], jnp.exp2(s - m).astype(BF16))

    def past_rest(h, j, raw):
        row_bias = maskb_ref[h, pl.ds(j, 1), :] + slope2[h] * ((j - i) * blk).astype(F32)
        m_old = m_ref[h]
        m_new = jnp.maximum(m_old, jnp.max(raw, axis=0, keepdims=True) + row_bias)
        m_ref[h] = m_new
        p = jnp.exp2(raw + (row_bias - m_new)).astype(BF16)
        acc_ref[h] = acc_ref[h] * jnp.exp2(m_old - m_new) + _dot(vt_ref[h, j], p)

    def blocks(js, with_own):
        tiles = ([(h, None) for h in heads] if with_own else []) + [(h, j) for j in js for h in heads]
        pipelined(tiles, lambda h, j: scores(h, i if j is None else j),
                  lambda h, j, raw: own_rest(h, raw) if j is None else past_rest(h, j, raw),
                  lead=MOBA_HEAD_TRIP_LEAD if with_own else MOBA_LOOP_LEAD)

    quads = i >> 2
    for rem in range(4):
        @pl.when((i & 3) == rem)
        def _():
            blocks([quads * 4 + d for d in range(rem)], with_own=True)

    def quad_body(jj, carry):
        blocks([4 * jj + d for d in range(4)], with_own=False)
        return carry

    lax.fori_loop(0, quads, quad_body, 0)

    for h in heads:
        acc = acc_ref[h]
        o_ref[:, lanes[h]] = (acc[:dh] / acc[dh:dh + 1]).T.astype(o_ref.dtype)


def _moba_attn(qkv, batch, seq):
    n_heads = MB_HEADS
    width = qkv.shape[1] // 3
    dh = width // n_heads
    blk = min(MB_BLOCK, seq)
    nb = seq // blk
    return pl.pallas_call(
        functools.partial(_moba_kernel, n_heads=n_heads, top_k=min(MB_TOPK, nb)),
        grid=(batch, nb),
        in_specs=[pl.BlockSpec((blk, width), lambda b, i: (b * nb + i, 0)),
                  pl.BlockSpec((seq, width), lambda b, i: (b, 1)),
                  pl.BlockSpec((seq, width), lambda b, i: (b, 2))],
        out_specs=pl.BlockSpec((blk, width), lambda b, i: (b * nb + i, 0)),
        out_shape=jax.ShapeDtypeStruct((batch * seq, width), BF16),
        scratch_shapes=[pltpu.VMEM((n_heads, nb, dh), BF16),
                        pltpu.VMEM((n_heads, nb, dh + MOBA_SUM_ROWS, blk), BF16),
                        pltpu.VMEM((n_heads, nb, blk), F32),
                        pltpu.VMEM((n_heads, blk, dh), BF16),
                        pltpu.VMEM((n_heads, blk, 2 * dh), BF16),
                        pltpu.VMEM((n_heads, 1, blk), F32),
                        pltpu.VMEM((n_heads, dh + MOBA_SUM_ROWS, blk), F32)],
        compiler_params=pltpu.CompilerParams(
            dimension_semantics=("parallel", "arbitrary"), vmem_limit_bytes=VMEM_LIMIT_BYTES),
        name="moba_attn",
    )(qkv, qkv, qkv)


def _pad_lanes(row_vec, offset):
    return jnp.zeros((1, LANES), F32).at[0, offset:offset + row_vec.shape[0]].set(row_vec.astype(F32))


def kernel(x, mix_norm_g, mlp_norm_g, dn_w_in, dn_conv_w, dn_a_log, dn_dt_bias, dn_out_norm_g, dn_w_out,
           mb_w_in, mb_w_out, mlp_w_up, mlp_w_down, final_norm_g):
    batch, seq, d = x.shape
    n = batch * seq
    h = x.reshape(n, d)
    dn_dim = DN_HEADS * DN_HEAD_DIM
    row = lambda v: v.reshape(1, -1).astype(F32)
    w_up, w_down = mlp_w_up.astype(BF16), mlp_w_down.astype(BF16)

    w_main = dn_w_in[0, :, :4 * dn_dim].astype(BF16)
    w_ba = jnp.zeros((d, LANES), F32).at[:, :2 * DN_HEADS].set(dn_w_in[0, :, 4 * dn_dim:]).astype(BF16)
    q, k, v, z, ba = _gdn_proj(h, row(mix_norm_g[0]), w_main, w_ba, dn_conv_w[0].astype(F32), seq)
    o = _gdn_core(q, k, v, z, ba, _pad_lanes(dn_a_log[0], DN_HEADS), _pad_lanes(dn_dt_bias[0], DN_HEADS),
                  row(dn_out_norm_g[0]), batch, seq)
    h = _out_mlp(o, h, dn_w_out[0].astype(BF16), row(mlp_norm_g[0]), w_up, w_down, 0,
                 row(final_norm_g), final_norm=False)

    (qkv,) = _rms_proj(h, row(mix_norm_g[1]), [mb_w_in[0].astype(BF16)], [BF16])
    o = _moba_attn(qkv, batch, seq)
    h = _out_mlp(o, h, mb_w_out[0].astype(BF16), row(mlp_norm_g[1]), w_up, w_down, 1,
                 row(final_norm_g), final_norm=True)
    return h.reshape(batch, seq, d)
```

```python
import functools
import math

import jax
import jax.numpy as jnp
from jax import lax
from jax.experimental import pallas as pl
from jax.experimental.pallas import tpu as pltpu

F32 = jnp.float32
BF16 = jnp.bfloat16

RMS_EPS = 1e-6
L2_EPS = 1e-6

DN_HEADS = 8
DN_HEAD_DIM = 128
DN_CONV = 4
GDN_CHUNK = 128
MB_HEADS = 8
MB_BLOCK = 256
MB_TOPK = 3

LANES = 128
SUBLANES = 8
VMEM_LIMIT_BYTES = 56 * 1024 * 1024

ROW_TILE = 1024
GDN_PROJ_ROW_TILE = 512
COL_TILE = 512
GDN_PROJ_COL_TILE = 256
GDN_TIME_TILE = 512
GDN_GROUP_CHUNKS = 2
MOBA_SUM_ROWS = 16
MOBA_HEAD_TRIP_LEAD = 6
MOBA_LOOP_LEAD = 4


def _dot(a, b):
    return jnp.dot(a, b, preferred_element_type=F32)


def _dot_nt(a, b):
    return lax.dot_general(a, b, (((1,), (1,)), ((), ())), preferred_element_type=F32)


def _dot_tn(a, b):
    return lax.dot_general(a, b, (((0,), (0,)), ((), ())), preferred_element_type=F32)


def _rms_scale(x):
    return x * lax.rsqrt(jnp.mean(x * x, axis=-1, keepdims=True) + RMS_EPS)


def _const_spec(shape):
    return pl.BlockSpec(shape, lambda *_: (0,) * len(shape), pipeline_mode=pl.Buffered(1))


def _layer_spec(stacked, layer):
    return pl.BlockSpec((None,) + stacked.shape[1:], lambda *_: (layer, 0, 0), pipeline_mode=pl.Buffered(1))


def _rms_proj_kernel(x_ref, g_ref, *refs, n_w, lead_cols, lead_scale):
    w_refs, o_refs = refs[:n_w], refs[n_w:]
    hn = (_rms_scale(x_ref[...]) * g_ref[...]).astype(BF16)
    for wi, (w_ref, o_ref) in enumerate(zip(w_refs, o_refs)):
        n_cols = w_ref.shape[1]
        tn = min(COL_TILE, n_cols)
        for c in range(n_cols // tn):
            cols = slice(c * tn, (c + 1) * tn)
            y = _dot(hn, w_ref[:, cols])
            if wi == 0 and (c + 1) * tn <= lead_cols:
                y = y * lead_scale
            o_ref[:, cols] = y.astype(o_ref.dtype)


def _rms_proj(x, g, ws, out_dtypes, lead_cols=0, lead_scale=1.0):
    n, d = x.shape
    tm = min(ROW_TILE, n)
    assert lead_cols % min(COL_TILE, ws[0].shape[1]) == 0
    in_specs = [pl.BlockSpec((tm, d), lambda i: (i, 0)), _const_spec((1, d))]
    in_specs += [_const_spec(w.shape) for w in ws]
    out_specs = [pl.BlockSpec((tm, w.shape[1]), lambda i: (i, 0)) for w in ws]
    out_shape = [jax.ShapeDtypeStruct((n, w.shape[1]), dt) for w, dt in zip(ws, out_dtypes)]
    return pl.pallas_call(
        functools.partial(_rms_proj_kernel, n_w=len(ws), lead_cols=lead_cols, lead_scale=lead_scale),
        grid=(n // tm,),
        in_specs=in_specs,
        out_specs=out_specs,
        out_shape=out_shape,
        compiler_params=pltpu.CompilerParams(
            dimension_semantics=("parallel",), vmem_limit_bytes=VMEM_LIMIT_BYTES),
        name="rms_proj",
    )(x, g, *ws)


def _gdn_proj_kernel(x_ref, g_ref, w_ref, wba_ref, cw_ref, q_ref, k_ref, v_ref, z_ref, ba_ref, halo_ref,
                     *, n_heads, tiles_per_seq):
    tm = x_ref.shape[0]
    width = q_ref.shape[1]
    dh = width // n_heads
    tc = min(GDN_PROJ_COL_TILE, width)
    groups = tm // SUBLANES
    hn = (_rms_scale(x_ref[...]) * g_ref[...]).astype(BF16)

    @pl.when(lax.rem(pl.program_id(0), tiles_per_seq) == 0)
    def _():
        halo_ref[...] = jnp.zeros_like(halo_ref)

    sub = lax.broadcasted_iota(jnp.int32, (groups, SUBLANES, tc), 1)
    for sec, o_ref in enumerate((q_ref, k_ref, v_ref)):
        for c in range(width // tc):
            cols = slice(sec * width + c * tc, sec * width + (c + 1) * tc)
            y = _dot(hn, w_ref[:, cols])
            yg = jnp.concatenate([y, halo_ref[:, cols]], axis=0).reshape(groups + 1, SUBLANES, tc)
            halo_ref[:, cols] = y[tm - SUBLANES:, :]
            acc = y * cw_ref[DN_CONV - 1:DN_CONV, cols]
            for s in range(1, DN_CONV):
                r = pltpu.roll(yg, s, axis=1)
                prev = jnp.concatenate([r[groups:], r[:groups - 1]], axis=0)
                shifted = jnp.where(sub >= s, r[:groups], prev).reshape(tm, tc)
                acc = acc + shifted * cw_ref[DN_CONV - 1 - s:DN_CONV - s, cols]
            a = acc * jax.nn.sigmoid(acc)
            if sec < 2:
                parts = []
                for hh in range(tc // dh):
                    p = a[:, hh * dh:(hh + 1) * dh]
                    inv = lax.rsqrt(jnp.sum(p * p, axis=-1, keepdims=True) + L2_EPS)
                    parts.append(p * (inv * dh ** -0.5 if sec == 0 else inv))
                a = jnp.concatenate(parts, axis=1)
            o_ref[:, c * tc:(c + 1) * tc] = a.astype(o_ref.dtype)
    for c in range(width // tc):
        z_ref[:, c * tc:(c + 1) * tc] = _dot(hn, w_ref[:, 3 * width + c * tc:3 * width + (c + 1) * tc])
    ba_ref[...] = _dot(hn, wba_ref[...])


def _gdn_proj(x, g, w_main, w_ba, conv_w, seq):
    n, d = x.shape
    width = w_main.shape[1] // 4
    tm = min(GDN_PROJ_ROW_TILE, seq)
    row = lambda i: (i, 0)
    return pl.pallas_call(
        functools.partial(_gdn_proj_kernel, n_heads=DN_HEADS, tiles_per_seq=seq // tm),
        grid=(n // tm,),
        in_specs=[pl.BlockSpec((tm, d), row), _const_spec((1, d)), _const_spec(w_main.shape),
                  _const_spec(w_ba.shape), _const_spec(conv_w.shape)],
        out_specs=[pl.BlockSpec((tm, width), row)] * 4 + [pl.BlockSpec((tm, LANES), row)],
        out_shape=[jax.ShapeDtypeStruct((n, width), BF16)] * 3
                  + [jax.ShapeDtypeStruct((n, width), F32), jax.ShapeDtypeStruct((n, LANES), F32)],
        scratch_shapes=[pltpu.VMEM((SUBLANES, 3 * width), F32)],
        compiler_params=pltpu.CompilerParams(
            dimension_semantics=("arbitrary",), vmem_limit_bytes=VMEM_LIMIT_BYTES),
        name="gdn_proj",
    )(x, g, w_main, w_ba, conv_w)


def _out_mlp_kernel(o_ref, h_ref, wout_ref, g_ref, wup_ref, wdn_ref, gf_ref, out_ref, *, final_norm):
    h1 = h_ref[...] + _dot(o_ref[...], wout_ref[...])
    hn = (_rms_scale(h1) * g_ref[...]).astype(BF16)
    out_ref[...] = h1
    d_ff = wup_ref.shape[1]
    tf = min(COL_TILE, d_ff)
    for c in range(d_ff // tf):
        cols = slice(c * tf, (c + 1) * tf)
        act = jnp.square(jnp.maximum(_dot(hn, wup_ref[:, cols]), 0.0)).astype(BF16)
        out_ref[...] += _dot(act, wdn_ref[cols, :])
    if final_norm:
        out_ref[...] = _rms_scale(out_ref[...]) * gf_ref[...]


def _out_mlp(o, h, w_out, g, w_up, w_down, layer, g_final, final_norm):
    n, d = h.shape
    tm = min(ROW_TILE, n)
    row = lambda i: (i, 0)
    return pl.pallas_call(
        functools.partial(_out_mlp_kernel, final_norm=final_norm),
        grid=(n // tm,),
        in_specs=[pl.BlockSpec((tm, o.shape[1]), row), pl.BlockSpec((tm, d), row),
                  _const_spec(w_out.shape), _const_spec((1, d)),
                  _layer_spec(w_up, layer), _layer_spec(w_down, layer), _const_spec((1, d))],
        out_specs=pl.BlockSpec((tm, d), row),
        out_shape=jax.ShapeDtypeStruct((n, d), F32),
        compiler_params=pltpu.CompilerParams(
            dimension_semantics=("parallel",), vmem_limit_bytes=VMEM_LIMIT_BYTES),
        name="out_mlp",
    )(o, h, w_out, g, w_up, w_down, g_final)


def _unit_lower_inverse_all(a_lows, row, col, between=()):
    c = a_lows[0].shape[0]
    eye = jnp.where(row == col, 1.0, 0.0).astype(F32)
    negs = [(-a).astype(BF16) for a in a_lows]
    between = list(between)
    ms = None
    s = 1
    while s < c:
        same_pair = (row // (2 * s)) == (col // (2 * s))
        off = same_pair & ((row % (2 * s)) >= s) & ((col % (2 * s)) < s)
        if ms is None:
            ms = [(eye - jnp.where(off, a, 0.0)).astype(BF16) for a in a_lows]
        else:
            ys = [_dot(m, n).astype(BF16) for m, n in zip(ms, negs)]
            ms = [jnp.where(off, _dot(y, m).astype(BF16), m) for y, m in zip(ys, ms)]
            if between:
                between.pop(0)()
        s *= 2
    for work in between:
        work()
    return ms


def _gdn_kernel(q_ref, k_ref, v_ref, z_ref, ba_ref, alog_ref, dtb_ref, ong_ref,
                o_ref, state_ref, *, n_heads, chunk):
    t_idx = pl.program_id(1)
    tt = q_ref.shape[0]
    dh = q_ref.shape[1] // n_heads
    n_chunks = tt // chunk

    @pl.when(t_idx == 0)
    def _():
        state_ref[...] = jnp.zeros_like(state_ref)

    ba = ba_ref[...]
    lane = lax.broadcasted_iota(jnp.int32, ba.shape, 1)
    trow = lax.broadcasted_iota(jnp.int32, ba.shape, 0)
    beta_all = jax.nn.sigmoid(ba)
    sp_in = ba + dtb_ref[...]
    softplus = jnp.maximum(sp_in, 0.0) + jnp.log(1.0 + jnp.exp(-jnp.abs(sp_in)))
    g_all = jnp.where((lane >= n_heads) & (lane < 2 * n_heads), -jnp.exp(alog_ref[...]) * softplus, 0.0)
    gc_all = g_all
    s = 1
    while s < chunk:
        shifted = pltpu.roll(gc_all, s, axis=0)
        gc_all = gc_all + jnp.where((trow % chunk) >= s, shifted, 0.0)
        s *= 2
    gc_all_t = gc_all.T
    beta_all_t = beta_all.T
    g_last_all = jnp.concatenate(
        [jnp.broadcast_to(gc_all[(c + 1) * chunk - 1:(c + 1) * chunk, :], (chunk, LANES)) for c in range(n_chunks)],
        axis=0)
    e_gc_all = jnp.exp(gc_all)
    e_rem_all = jnp.exp(g_last_all - gc_all)
    e_last_all = jnp.exp(g_last_all)

    row = lax.broadcasted_iota(jnp.int32, (chunk, chunk), 0)
    col = lax.broadcasted_iota(jnp.int32, (chunk, chunk), 1)

    rows_of = lambda c: slice(c * chunk, (c + 1) * chunk)
    lanes_of = lambda h: slice(h * dh, (h + 1) * dh)
    heads = range(n_heads)

    def recurrence_steps(c, k16, q16, attn16, sol):
        rows = rows_of(c)
        st = {}

        def read_state():
            st["states"] = [state_ref[h] for h in heads]
            st["s16"] = [s.astype(BF16) for s in st["states"]]
            st["wq"] = [_dot(jnp.concatenate([sol[h][:, dh:].astype(BF16), q16[h]], axis=0), st["s16"][h])
                        for h in heads]

        def new_values():
            st["v_new"] = [sol[h][:, :dh] - st["wq"][h][:chunk] for h in heads]
            vn16 = [v.astype(BF16) for v in st["v_new"]]
            st["o_inter"] = [wq[chunk:] for wq in st["wq"]]
            kd_t = [(k16[h].astype(F32) * e_rem_all[rows, n_heads + h:n_heads + h + 1]).T.astype(BF16)
                    for h in heads]
            both = [_dot(jnp.concatenate([attn16[h], kd_t[h]], axis=0), vn16[h]) for h in heads]
            st["o_intra"] = [b[:chunk] for b in both]
            st["s_upd"] = [b[chunk:] for b in both]

        def write_out():
            for h in heads:
                gl = n_heads + h
                state_ref[h] = st["states"][h] * e_last_all[c * chunk:c * chunk + 1, gl:gl + 1] + st["s_upd"][h]
                z = z_ref[rows, lanes_of(h)]
                o = st["o_inter"][h] * e_gc_all[rows, gl:gl + 1] + st["o_intra"][h]
                o = _rms_scale(o) * ong_ref[...] * (z * jax.nn.sigmoid(z))
                o_ref[rows, lanes_of(h)] = o.astype(o_ref.dtype)

        return [read_state, new_values, write_out]

    group_chunks = [list(range(c0, min(c0 + GDN_GROUP_CHUNKS, n_chunks)))
                    for c0 in range(0, n_chunks, GDN_GROUP_CHUNKS)]
    group_pairs = [[(c, h) for c in cs for h in heads] for cs in group_chunks]
    k16 = [[k_ref[rows_of(c), lanes_of(h)] for c, h in pairs] for pairs in group_pairs]
    q16 = [[q_ref[rows_of(c), lanes_of(h)] for c, h in pairs] for pairs in group_pairs]
    kq = [[_dot_nt(jnp.concatenate([k, q], axis=0), k) for q, k in zip(qs, ks)] for qs, ks in zip(q16, k16)]
    pending = []
    for g, pairs in enumerate(group_pairs):
        a_low, attn16 = [], []
        for (c, h), kq_p in zip(pairs, kq[g]):
            kk_p, qk_p = kq_p[:chunk], kq_p[chunk:]
            gl = n_heads + h
            gc = gc_all[rows_of(c), gl:gl + 1]
            gc_row = gc_all_t[gl:gl + 1, rows_of(c)]
            decay = jnp.exp(jnp.where(row >= col, gc - gc_row, -jnp.inf))
            a_low.append(jnp.where(row > col, kk_p * (decay * beta_all[rows_of(c), h:h + 1]), 0.0))
            attn16.append((qk_p * decay).astype(BF16))
        t_inv = _unit_lower_inverse_all(a_low, row, col, between=pending)
        sol = []
        for (c, h), t, k in zip(pairs, t_inv, k16[g]):
            gl = n_heads + h
            t_beta = (t.astype(F32) * beta_all_t[h:h + 1, rows_of(c)]).astype(BF16)
            k_dec = (k.astype(F32) * e_gc_all[rows_of(c), gl:gl + 1]).astype(BF16)
            sol.append(_dot(t_beta, jnp.concatenate([v_ref[rows_of(c), lanes_of(h)], k_dec], axis=-1)))
        pending = []
        for ci, c in enumerate(group_chunks[g]):
            part = slice(ci * n_heads, (ci + 1) * n_heads)
            pending += recurrence_steps(c, k16[g][part], q16[g][part], attn16[part], sol[part])
    for work in pending:
        work()


def _gdn_core(q, k, v, z, ba, a_log_row, dt_bias_row, out_norm_g, batch, seq):
    n_heads, dh = DN_HEADS, DN_HEAD_DIM
    width = n_heads * dh
    tt = min(GDN_TIME_TILE, seq)
    nt = seq // tt
    tile = lambda b, t: (b * nt + t, 0)
    return pl.pallas_call(
        functools.partial(_gdn_kernel, n_heads=n_heads, chunk=min(GDN_CHUNK, tt)),
        grid=(batch, nt),
        in_specs=[pl.BlockSpec((tt, width), tile)] * 4 + [pl.BlockSpec((tt, LANES), tile),
                  _const_spec((1, LANES)), _const_spec((1, LANES)), _const_spec((1, dh))],
        out_specs=pl.BlockSpec((tt, width), tile),
        out_shape=jax.ShapeDtypeStruct((batch * seq, width), BF16),
        scratch_shapes=[pltpu.VMEM((n_heads, dh, dh), F32)],
        compiler_params=pltpu.CompilerParams(
            dimension_semantics=("parallel", "arbitrary"), vmem_limit_bytes=VMEM_LIMIT_BYTES),
        name="gdn_core",
    )(q, k, v, z, ba, a_log_row, dt_bias_row, out_norm_g)


def _split3_f32(x):
    hi = x.astype(BF16).astype(F32)
    mid = (x - hi).astype(BF16).astype(F32)
    return hi, mid, ((x - hi) - mid).astype(BF16).astype(F32)


def _moba_kernel(q_ref, k_ref, v_ref, o_ref, kmean_ref, vt_ref, maskb_ref, kaug_ref, qaug_ref,
                 m_ref, acc_ref, *, n_heads, top_k):
    i = pl.program_id(1)
    blk = q_ref.shape[0]
    seq = k_ref.shape[0]
    dh = k_ref.shape[1] // n_heads
    nb = seq // blk
    heads = range(n_heads)
    lanes = [slice(h * dh, (h + 1) * dh) for h in heads]
    slope2 = [2.0 ** (-8.0 * (h + 1) / n_heads) * math.log2(math.e) for h in heads]

    krow = lax.broadcasted_iota(jnp.int32, (blk, blk), 0)
    qcol = lax.broadcasted_iota(jnp.int32, (blk, blk), 1)
    aug_lane = lax.broadcasted_iota(jnp.int32, (blk, dh), 1)

    @pl.when(i == 0)
    def _():
        r = lax.broadcasted_iota(jnp.int32, (nb, seq), 0)
        c = lax.broadcasted_iota(jnp.int32, (nb, seq), 1)
        ind = jnp.where(c // blk == r, 1.0, 0.0).astype(BF16)
        key_pos = lax.broadcasted_iota(jnp.int32, (blk, dh), 0).astype(F32)
        for h in heads:
            kmean_ref[h] = (_dot(ind, k_ref[:, lanes[h]]) * (1.0 / blk)).astype(BF16)
            hi, mid, lo = _split3_f32(slope2[h] * key_pos)
            kaug_ref[h] = jnp.where(aug_lane == 0, hi, jnp.where(aug_lane == 1, mid,
                                                                 jnp.where(aug_lane == 2, lo, 0.0))).astype(BF16)
            for j in range(nb):
                vt = v_ref[j * blk:(j + 1) * blk, lanes[h]].astype(F32).T.astype(BF16)
                vt_ref[h, j] = jnp.concatenate([vt, jnp.ones((MOBA_SUM_ROWS, blk), BF16)], axis=0)

    qs = [q_ref[:, lanes[h]] for h in heads]
    ones_cols = jnp.where(aug_lane < 3, 1.0, 0.0).astype(BF16)
    for h in heads:
        qaug_ref[h] = jnp.concatenate([qs[h], ones_cols], axis=1)

    gates = [_dot_nt(kmean_ref[h], qs[h]) for h in heads]
    bidx = lax.broadcasted_iota(jnp.int32, (nb, blk), 0)
    past = bidx < i
    for h in heads:
        gate = jnp.where(past, gates[h], -jnp.inf)
        rank = jnp.zeros(gate.shape, F32)
        for jp in range(nb):
            other = gate[jp:jp + 1, :]
            beats = (other > gate) | ((other == gate) & (jp < bidx))
            rank = rank + jnp.where(beats, 1.0, 0.0)
        maskb_ref[h] = jnp.where(past & (rank < top_k), 0.0, -jnp.inf)

    def pipelined(tiles, score_mm, rest, lead):
        raw = {}
        for step in range(len(tiles) + lead):
            if step < len(tiles):
                raw[step] = score_mm(*tiles[step])
            if step >= lead:
                rest(*tiles[step - lead], raw.pop(step - lead))

    def scores(h, j):
        rows = pl.ds(pl.multiple_of(j * blk, blk), blk)
        k_aug = jnp.concatenate([k_ref[rows, lanes[h]], kaug_ref[h]], axis=1)
        return _dot_nt(k_aug, qaug_ref[h])

    def own_rest(h, raw):
        s = jnp.where(krow <= qcol, raw, -jnp.inf)
        m = jnp.max(s, axis=0, keepdims=True)
        m_ref[h] = m
        acc_ref[h] = _dot(vt_ref[h, i], jnp.exp2(s - m).astype(BF16))

    def past_rest(h, j, raw):
        row_bias = maskb_ref[h, pl.ds(j, 1), :] + slope2[h] * ((j - i) * blk).astype(F32)
        m_old = m_ref[h]
        m_new = jnp.maximum(m_old, jnp.max(raw, axis=0, keepdims=True) + row_bias)
        m_ref[h] = m_new
        p = jnp.exp2(raw + (row_bias - m_new)).astype(BF16)
        acc_ref[h] = acc_ref[h] * jnp.exp2(m_old - m_new) + _dot(vt_ref[h, j], p)

    def blocks(js, with_own):
        tiles = ([(h, None) for h in heads] if with_own else []) + [(h, j) for j in js for h in heads]
        pipelined(tiles, lambda h, j: scores(h, i if j is None else j),
                  lambda h, j, raw: own_rest(h, raw) if j is None else past_rest(h, j, raw),
                  lead=MOBA_HEAD_TRIP_LEAD if with_own else MOBA_LOOP_LEAD)

    quads = i >> 2
    for rem in range(4):
        @pl.when((i & 3) == rem)
        def _():
            blocks([quads * 4 + d for d in range(rem)], with_own=True)

    def quad_body(jj, carry):
        blocks([4 * jj + d for d in range(4)], with_own=False)
        return carry

    lax.fori_loop(0, quads, quad_body, 0)

    for h in heads:
        acc = acc_ref[h]
        o_ref[:, lanes[h]] = (acc[:dh] / acc[dh:dh + 1]).T.astype(o_ref.dtype)


def _moba_q_scale(dh):
    return dh ** -0.5 * math.log2(math.e)


def _moba_attn(qkv, batch, seq):
    n_heads = MB_HEADS
    width = qkv.shape[1] // 3
    dh = width // n_heads
    blk = min(MB_BLOCK, seq)
    nb = seq // blk
    return pl.pallas_call(
        functools.partial(_moba_kernel, n_heads=n_heads, top_k=min(MB_TOPK, nb)),
        grid=(batch, nb),
        in_specs=[pl.BlockSpec((blk, width), lambda b, i: (b * nb + i, 0)),
                  pl.BlockSpec((seq, width), lambda b, i: (b, 1)),
                  pl.BlockSpec((seq, width), lambda b, i: (b, 2))],
        out_specs=pl.BlockSpec((blk, width), lambda b, i: (b * nb + i, 0)),
        out_shape=jax.ShapeDtypeStruct((batch * seq, width), BF16),
        scratch_shapes=[pltpu.VMEM((n_heads, nb, dh), BF16),
                        pltpu.VMEM((n_heads, nb, dh + MOBA_SUM_ROWS, blk), BF16),
                        pltpu.VMEM((n_heads, nb, blk), F32),
                        pltpu.VMEM((n_heads, blk, dh), BF16),
                        pltpu.VMEM((n_heads, blk, 2 * dh), BF16),
                        pltpu.VMEM((n_heads, 1, blk), F32),
                        pltpu.VMEM((n_heads, dh + MOBA_SUM_ROWS, blk), F32)],
        compiler_params=pltpu.CompilerParams(
            dimension_semantics=("parallel", "arbitrary"), vmem_limit_bytes=VMEM_LIMIT_BYTES),
        name="moba_attn",
    )(qkv, qkv, qkv)


def _pad_lanes(row_vec, offset):
    return jnp.zeros((1, LANES), F32).at[0, offset:offset + row_vec.shape[0]].set(row_vec.astype(F32))


def kernel(x, mix_norm_g, mlp_norm_g, dn_w_in, dn_conv_w, dn_a_log, dn_dt_bias, dn_out_norm_g, dn_w_out,
           mb_w_in, mb_w_out, mlp_w_up, mlp_w_down, final_norm_g):
    batch, seq, d = x.shape
    n = batch * seq
    h = x.reshape(n, d)
    dn_dim = DN_HEADS * DN_HEAD_DIM
    row = lambda v: v.reshape(1, -1).astype(F32)
    w_up, w_down = mlp_w_up.astype(BF16), mlp_w_down.astype(BF16)

    w_main = dn_w_in[0, :, :4 * dn_dim].astype(BF16)
    w_ba = jnp.zeros((d, LANES), F32).at[:, :2 * DN_HEADS].set(dn_w_in[0, :, 4 * dn_dim:]).astype(BF16)
    q, k, v, z, ba = _gdn_proj(h, row(mix_norm_g[0]), w_main, w_ba, dn_conv_w[0].astype(F32), seq)
    o = _gdn_core(q, k, v, z, ba, _pad_lanes(dn_a_log[0], DN_HEADS), _pad_lanes(dn_dt_bias[0], DN_HEADS),
                  row(dn_out_norm_g[0]), batch, seq)
    h = _out_mlp(o, h, dn_w_out[0].astype(BF16), row(mlp_norm_g[0]), w_up, w_down, 0,
                 row(final_norm_g), final_norm=False)

    mb_dim = mb_w_in.shape[2] // 3
    (qkv,) = _rms_proj(h, row(mix_norm_g[1]), [mb_w_in[0].astype(BF16)], [BF16],
                       lead_cols=mb_dim, lead_scale=_moba_q_scale(mb_dim // MB_HEADS))
    o = _moba_attn(qkv, batch, seq)
    h = _out_mlp(o, h, mb_w_out[0].astype(BF16), row(mlp_norm_g[1]), w_up, w_down, 1,
                 row(final_norm_g), final_norm=True)
    return h.reshape(batch, seq, d)
```
